```python
import math
import jax, jax.numpy as jnp
from jax import lax
import numpy as np

D_MODEL = 4096
BATCH = 4
SEQ = 2048
DEPTH = 1
DEC_BATCH = 1
DEC_SEQ = 16384
PAST_LEN = 128

HEAD_DIM = 128
DIFF_HEADS = D_MODEL // (2 * HEAD_DIM)
QK_WIDTH = DIFF_HEADS * 2 * HEAD_DIM
V_WIDTH = DIFF_HEADS * 2 * HEAD_DIM
CONV_DIM = D_MODEL
CONV_WIDTH = 3
FFN_DIM = ((8 * D_MODEL // 3 + 255) // 256) * 256
N_SUBLAYERS = 3
ROPE_THETA = 10000.0
Q_BLOCK = 128
NORM_EPS = 1e-6
IN_COLS = 2 * QK_WIDTH + V_WIDTH + 3 * CONV_DIM + 2 * D_MODEL
SPLIT_POINTS = [QK_WIDTH, 2 * QK_WIDTH, 2 * QK_WIDTH + V_WIDTH,
                2 * QK_WIDTH + V_WIDTH + CONV_DIM,
                2 * QK_WIDTH + V_WIDTH + 2 * CONV_DIM,
                2 * QK_WIDTH + V_WIDTH + 3 * CONV_DIM,
                2 * QK_WIDTH + V_WIDTH + 3 * CONV_DIM + D_MODEL]

kernel_name = "hybrid_diffattn_shortconv_macaron_encoder"


def rmsnorm(x, g):
    xf = x.astype(jnp.float32)
    xf = xf * lax.rsqrt(jnp.mean(xf * xf, axis=-1, keepdims=True) + NORM_EPS)
    return (xf * g.astype(jnp.float32)).astype(x.dtype)


def rope(x):
    seq = x.shape[1]
    inv_freq = ROPE_THETA ** (-jnp.arange(0, HEAD_DIM, 2, dtype=jnp.float32) / HEAD_DIM)
    ang = jnp.arange(seq, dtype=jnp.float32)[:, None] * inv_freq[None, :]
    cos = jnp.cos(ang)[None, :, None, None, :]
    sin = jnp.sin(ang)[None, :, None, None, :]
    xf = x.astype(jnp.float32)
    x1, x2 = xf[..., :HEAD_DIM // 2], xf[..., HEAD_DIM // 2:]
    out = jnp.concatenate([x1 * cos - x2 * sin, x2 * cos + x1 * sin], axis=-1)
    return out.astype(x.dtype)


def diff_attention(q, k, v, lam):
    bsz, seq = q.shape[0], q.shape[1]
    nb = seq // Q_BLOCK
    scale = 1.0 / math.sqrt(HEAD_DIM)
    qb = (q * scale).reshape(bsz, nb, Q_BLOCK, DIFF_HEADS, 2, HEAD_DIM).transpose(1, 0, 2, 3, 4, 5)

    def block(qi):
        s = jnp.einsum('bqhcd,bkhcd->bhcqk', qi, k, preferred_element_type=jnp.float32)
        p = jax.nn.softmax(s, axis=-1)
        p = p[:, :, 0] - lam * p[:, :, 1]
        return jnp.einsum('bhqk,bkhe->bqhe', p.astype(v.dtype), v)

    o = lax.map(block, qb)
    return o.transpose(1, 0, 2, 3, 4).reshape(bsz, seq, DIFF_HEADS, 2 * HEAD_DIM)


def swiglu(h, w_in, w_out):
    a = h @ w_in
    gate, up = a[..., :FFN_DIM], a[..., FFN_DIM:]
    return (jax.nn.silu(gate) * up) @ w_out


def token_mixer(h, w_in, conv_w, lambda_qk, g_subln, w_branch_attn, w_branch_conv, w_out, lam_init):
    bsz, seq, _ = h.shape
    proj = h @ w_in
    q, k, v, b_gate, c_gate, x_conv, g_attn, g_conv = jnp.split(proj, SPLIT_POINTS, axis=-1)

    q = rope(q.reshape(bsz, seq, DIFF_HEADS, 2, HEAD_DIM))
    k = rope(k.reshape(bsz, seq, DIFF_HEADS, 2, HEAD_DIM))
    v = v.reshape(bsz, seq, DIFF_HEADS, 2 * HEAD_DIM)
    lq = lambda_qk.astype(jnp.float32)
    lam = jnp.exp(jnp.sum(lq[0] * lq[1])) - jnp.exp(jnp.sum(lq[2] * lq[3])) + lam_init
    o = diff_attention(q, k, v, lam)
    o = rmsnorm(o, g_subln) * (1.0 - lam_init)
    attn_branch = o.reshape(bsz, seq, V_WIDTH) @ w_branch_attn

    u = c_gate * x_conv
    up = jnp.pad(u, ((0, 0), (1, 1), (0, 0)))
    y = up[:, :-2] * conv_w[0] + up[:, 1:-1] * conv_w[1] + up[:, 2:] * conv_w[2]
    conv_branch = (b_gate * y) @ w_branch_conv

    merged = jax.nn.sigmoid(g_attn) * attn_branch + jax.nn.sigmoid(g_conv) * conv_branch
    return merged @ w_out


def encoder(x, c, w_mod, b_mod, g_norm, w_ffn_in, w_ffn_out, w_in, conv_w, lambda_qk,
            g_subln, w_branch_attn, w_branch_conv, w_out, g_final):
    bsz = x.shape[0]
    for l in range(DEPTH):
        lam_init = 0.8 - 0.6 * math.exp(-0.3 * l)
        mod = (c @ w_mod[l] + b_mod[l]).reshape(bsz, N_SUBLAYERS, 3, D_MODEL)
        shift, scale, gate = mod[:, :, 0, None, :], mod[:, :, 1, None, :], mod[:, :, 2, None, :]
        h = rmsnorm(x, g_norm[l, 0]) * (1.0 + scale[:, 0]) + shift[:, 0]
        x = x + 0.5 * gate[:, 0] * swiglu(h, w_ffn_in[l, 0], w_ffn_out[l, 0])
        h = rmsnorm(x, g_norm[l, 1]) * (1.0 + scale[:, 1]) + shift[:, 1]
        x = x + gate[:, 1] * token_mixer(h, w_in[l], conv_w[l], lambda_qk[l], g_subln[l],
                                          w_branch_attn[l], w_branch_conv[l], w_out[l], lam_init)
        h = rmsnorm(x, g_norm[l, 2]) * (1.0 + scale[:, 2]) + shift[:, 2]
        x = x + 0.5 * gate[:, 2] * swiglu(h, w_ffn_in[l, 1], w_ffn_out[l, 1])
    return rmsnorm(x, g_final)


def setup_inputs(seed: int = 0) -> dict:
    key = jax.random.key(seed)
    ks = jax.random.split(key, 20)
    f32 = jnp.float32
    nrm = lambda k, shape, s: jax.random.normal(k, shape, f32) * s
    return {
        "x_prompt": nrm(ks[0], (BATCH, SEQ, D_MODEL), 1.0),
        "x_sample": nrm(ks[1], (DEC_BATCH, DEC_SEQ, D_MODEL), 1.0),
        "c_prompt": nrm(ks[2], (BATCH, D_MODEL), 1.0),
        "c_sample": nrm(ks[3], (DEC_BATCH, D_MODEL), 1.0),
        "w_mod": nrm(ks[4], (DEPTH, D_MODEL, N_SUBLAYERS * 3 * D_MODEL), 0.5 * D_MODEL ** -0.5),
        "b_mod": nrm(ks[5], (DEPTH, N_SUBLAYERS * 3 * D_MODEL), 0.01),
        "g_norm": 1.0 + nrm(ks[6], (DEPTH, N_SUBLAYERS, D_MODEL), 0.02),
        "w_ffn_in": nrm(ks[7], (DEPTH, 2, D_MODEL, 2 * FFN_DIM), D_MODEL ** -0.5),
        "w_ffn_out": nrm(ks[8], (DEPTH, 2, FFN_DIM, D_MODEL), FFN_DIM ** -0.5),
        "w_in": nrm(ks[9], (DEPTH, D_MODEL, IN_COLS), D_MODEL ** -0.5),
        "conv_w": nrm(ks[10], (DEPTH, CONV_WIDTH, CONV_DIM), CONV_WIDTH ** -0.5),
        "lambda_qk": nrm(ks[11], (DEPTH, 4, HEAD_DIM), 0.1),
        "g_subln": 1.0 + nrm(ks[12], (DEPTH, 2 * HEAD_DIM), 0.02),
        "w_branch_attn": nrm(ks[13], (DEPTH, V_WIDTH, D_MODEL), V_WIDTH ** -0.5),
        "w_branch_conv": nrm(ks[14], (DEPTH, CONV_DIM, D_MODEL), CONV_DIM ** -0.5),
        "w_out": nrm(ks[15], (DEPTH, D_MODEL, D_MODEL), D_MODEL ** -0.5),
        "g_final": 1.0 + nrm(ks[16], (D_MODEL,), 0.02),
    }


def reference(x_prompt, x_sample, c_prompt, c_sample, w_mod, b_mod, g_norm, w_ffn_in, w_ffn_out,
              w_in, conv_w, lambda_qk, g_subln, w_branch_attn, w_branch_conv, w_out, g_final):
    y_prompt = encoder(x_prompt, c_prompt, w_mod, b_mod, g_norm, w_ffn_in, w_ffn_out, w_in, conv_w,
                       lambda_qk, g_subln, w_branch_attn, w_branch_conv, w_out, g_final)
    y_sample = encoder(x_sample, c_sample, w_mod, b_mod, g_norm, w_ffn_in, w_ffn_out, w_in, conv_w,
                       lambda_qk, g_subln, w_branch_attn, w_branch_conv, w_out, g_final)
    return (y_prompt, y_sample)
```

```python
import functools
import math

import jax
import jax.numpy as jnp
from jax import lax
from jax.experimental import pallas as pl
from jax.experimental.pallas import tpu as pltpu

HEAD_DIM = 128
NORM_EPS = 1e-6
ROPE_THETA = 10000.0
N_SUBLAYERS = 3
LANES = 128
SUBLANES = 8
VMEM_LIMIT_BYTES = 60 * 1024 * 1024
FFN_PAD = 1024

F32 = jnp.float32
BF16 = jnp.bfloat16


def _params(n_grid_dims):
    return pltpu.CompilerParams(
        dimension_semantics=("arbitrary",) * n_grid_dims,
        vmem_limit_bytes=VMEM_LIMIT_BYTES,
    )


def _round_up(x, m):
    return (x + m - 1) // m * m


def _mod_kernel(c_ref, w_ref, b_ref, o_ref):
    o_ref[...] = jnp.dot(c_ref[...].astype(BF16), w_ref[...].astype(BF16),
                         preferred_element_type=F32) + b_ref[...]


def _modulation(c_all, w_mod, b_mod):
    rows, d = c_all.shape
    n = w_mod.shape[1]
    tn = min(512, n)
    return pl.pallas_call(
        _mod_kernel,
        grid=(n // tn,),
        in_specs=[pl.BlockSpec((rows, d), lambda j: (0, 0)),
                  pl.BlockSpec((d, tn), lambda j: (0, j)),
                  pl.BlockSpec((1, tn), lambda j: (0, j))],
        out_specs=pl.BlockSpec((rows, tn), lambda j: (0, j)),
        out_shape=jax.ShapeDtypeStruct((rows, n), F32),
        compiler_params=_params(1),
        name="modulation",
    )(c_all, w_mod, b_mod.reshape(1, n))


def _norm_mod_kernel(x_ref, g_ref, scale_ref, shift_ref, o_ref):
    x = x_ref[...]
    ms = jnp.mean(x * x, axis=-1, keepdims=True)
    xn = (x * lax.rsqrt(ms + NORM_EPS)) * g_ref[...]
    o_ref[...] = (xn * (1.0 + scale_ref[...]) + shift_ref[...]).astype(o_ref.dtype)


def _norm_kernel(x_ref, g_ref, o_ref):
    x = x_ref[...]
    ms = jnp.mean(x * x, axis=-1, keepdims=True)
    o_ref[...] = ((x * lax.rsqrt(ms + NORM_EPS)) * g_ref[...]).astype(o_ref.dtype)


def _norm_mod(x, g, scale, shift, seq):
    m, d = x.shape
    tm = min(512, seq)
    vec = pl.BlockSpec((None, 1, d), lambda i: (i * tm // seq, 0, 0))
    return pl.pallas_call(
        _norm_mod_kernel,
        grid=(m // tm,),
        in_specs=[pl.BlockSpec((tm, d), lambda i: (i, 0)),
                  pl.BlockSpec((1, d), lambda i: (0, 0)), vec, vec],
        out_specs=pl.BlockSpec((tm, d), lambda i: (i, 0)),
        out_shape=jax.ShapeDtypeStruct((m, d), BF16),
        compiler_params=_params(1),
        name="norm_mod",
    )(x, g.reshape(1, d), scale, shift)


def _final_norm(x, g, seq):
    m, d = x.shape
    tm = min(512, seq)
    return pl.pallas_call(
        _norm_kernel,
        grid=(m // tm,),
        in_specs=[pl.BlockSpec((tm, d), lambda i: (i, 0)),
                  pl.BlockSpec((1, d), lambda i: (0, 0))],
        out_specs=pl.BlockSpec((tm, d), lambda i: (i, 0)),
        out_shape=jax.ShapeDtypeStruct((m, d), F32),
        compiler_params=_params(1),
        name="final_norm",
    )(x, g.reshape(1, d))


def _mm_body(*refs, n_a, n_b, n_e, n_o, nk, pair_a, epilogue):
    a = refs[:n_a]
    b = refs[n_a:n_a + n_b]
    e = refs[n_a + n_b:n_a + n_b + n_e]
    o = refs[n_a + n_b + n_e:n_a + n_b + n_e + n_o]
    acc = refs[n_a + n_b + n_e + n_o:]
    ids = (pl.program_id(0), pl.program_id(1))
    r = [jnp.dot(a[pair_a[t]][...], b[t][...], preferred_element_type=F32) for t in range(n_b)]
    if nk == 1:
        epilogue(ids, r, e, o)
        return
    k = pl.program_id(2)

    @pl.when(k == 0)
    def _():
        for t in range(n_b):
            acc[t][...] = r[t]

    @pl.when(jnp.logical_and(k > 0, k < nk - 1))
    def _():
        for t in range(n_b):
            acc[t][...] += r[t]

    @pl.when(k == nk - 1)
    def _():
        epilogue(ids, [acc[t][...] + r[t] for t in range(n_b)], e, o)


def _matmul(a_arrs, b_arrs, extras, outs, epilogue, *, m, n, kdim, tm, tn, tk,
            pair_a=None, b_col_off=None, name):
    n_b = len(b_arrs)
    pair_a = tuple(pair_a) if pair_a is not None else (0,) * n_b
    b_col_off = tuple(b_col_off) if b_col_off is not None else (0,) * n_b
    assert m % tm == 0 and n % tn == 0 and kdim % tk == 0
    nk = kdim // tk
    in_specs = [pl.BlockSpec((tm, tk), lambda i, j, k: (i, k)) for _ in a_arrs]
    for off in b_col_off:
        in_specs.append(pl.BlockSpec((tk, tn), functools.partial(
            lambda i, j, k, off: (k, j + off), off=off)))
    in_specs += [s for _, s in extras]
    scratch = [pltpu.VMEM((tm, tn), F32) for _ in range(n_b)] if nk > 1 else []
    body = functools.partial(_mm_body, n_a=len(a_arrs), n_b=n_b, n_e=len(extras), n_o=len(outs),
                             nk=nk, pair_a=pair_a, epilogue=epilogue)
    res = pl.pallas_call(
        body,
        grid=(m // tm, n // tn, nk),
        in_specs=in_specs,
        out_specs=[s for _, s in outs],
        out_shape=[sh for sh, _ in outs],
        scratch_shapes=scratch,
        compiler_params=_params(3),
        name=name,
    )(*a_arrs, *b_arrs, *[x for x, _ in extras])
    return res


def _tile_spec(tm, tn, col_off=0):
    return pl.BlockSpec((tm, tn), lambda i, j, k: (i, j + col_off))


def _seq_vec_spec(tm, tn, seq):
    return pl.BlockSpec((None, 1, tn), lambda i, j, k: (i * tm // seq, 0, j))


def _epi_swiglu(ids, r, e, o):
    g, u = r
    o[0][...] = ((g * jax.nn.sigmoid(g)) * u).astype(o[0].dtype)


def _epi_residual(coef, ids, r, e, o):
    x_ref, gate_ref = e
    gate = gate_ref[...] if coef == 1.0 else coef * gate_ref[...]
    o[0][...] = x_ref[...] + gate * r[0]


def _epi_rope(n_q_tiles, q_scale, tn, ids, r, e, o):
    cos = e[0][...]
    sin = e[1][...]
    s = jnp.where(ids[1] < n_q_tiles, q_scale, 1.0).astype(F32)
    acc = r[0]
    for g in range(tn // HEAD_DIM):
        blk = acc[:, g * HEAD_DIM:(g + 1) * HEAD_DIM]
        rot = pltpu.roll(blk, HEAD_DIM // 2, 1)
        o[0][:, g * HEAD_DIM:(g + 1) * HEAD_DIM] = ((blk * cos + rot * sin) * s).astype(o[0].dtype)


def _epi_cast(ids, r, e, o):
    o[0][...] = r[0].astype(o[0].dtype)


def _epi_conv_operands(ids, r, e, o):
    o[0][...] = r[0]
    o[1][...] = r[1] * r[2]


def _epi_sigmoid(ids, r, e, o):
    o[0][...] = jax.nn.sigmoid(r[0])


def _epi_merge(ids, r, e, o):
    o[0][...] = (e[0][...] * r[0] + e[1][...] * r[1]).astype(o[0].dtype)


def _attn_kernel(lq_ref, g_ref, q_ref, k_ref, v_ref, o_ref, acc_ref, *, tk, lam_init):
    tq = q_ref.shape[0]
    seq = k_ref.shape[0]
    nk = seq // tk
    q0 = q_ref[:, :HEAD_DIM]
    q1 = q_ref[:, HEAD_DIM:]
    acc_ref[...] = jnp.zeros_like(acc_ref)

    def one(c, q, kc, vc, m, l):
        s = lax.dot_general(q, kc, (((1,), (1,)), ((), ())), preferred_element_type=F32)
        m_new = jnp.maximum(m, jnp.max(s, axis=-1, keepdims=True))
        alpha = jnp.exp(m - m_new)
        p = jnp.exp(s - m_new)
        l_new = alpha * l + jnp.sum(p, axis=-1, keepdims=True)
        acc_ref[c] = acc_ref[c] * alpha + jnp.dot(p.astype(BF16), vc, preferred_element_type=F32)
        return m_new, l_new

    def body(j, carry):
        m0, l0, m1, l1 = carry
        start = pl.multiple_of(j * tk, tk)
        kblk = k_ref[pl.ds(start, tk), :]
        vblk = v_ref[pl.ds(start, tk), :]
        m0, l0 = one(0, q0, kblk[:, :HEAD_DIM], vblk, m0, l0)
        m1, l1 = one(1, q1, kblk[:, HEAD_DIM:], vblk, m1, l1)
        return m0, l0, m1, l1

    neg = jnp.full((tq, 1), -jnp.inf, F32)
    zero = jnp.zeros((tq, 1), F32)
    _, l0, _, l1 = lax.fori_loop(0, nk, body, (neg, zero, neg, zero))

    lq = lq_ref[...]
    lam = (jnp.exp(jnp.sum(lq[0:1] * lq[1:2], axis=-1, keepdims=True))
           - jnp.exp(jnp.sum(lq[2:3] * lq[3:4], axis=-1, keepdims=True)) + lam_init)
    o = acc_ref[0] / l0 - lam * (acc_ref[1] / l1)
    ms = jnp.mean(o * o, axis=-1, keepdims=True)
    o = (o * lax.rsqrt(ms + NORM_EPS)) * g_ref[...]
    o_ref[...] = (o * (1.0 - lam_init)).astype(o_ref.dtype)


def _attention(qk, v, lambda_qk, g_subln, *, bsz, seq, d, lam_init):
    heads = d // (2 * HEAD_DIM)
    hw = 2 * HEAD_DIM
    tq = min(1024, seq)
    tk = min(512, seq)
    nq = seq // tq
    kern = functools.partial(_attn_kernel, tk=tk, lam_init=lam_init)
    return pl.pallas_call(
        kern,
        grid=(bsz, heads, nq),
        in_specs=[pl.BlockSpec((4, HEAD_DIM), lambda b, h, i: (0, 0)),
                  pl.BlockSpec((1, hw), lambda b, h, i: (0, 0)),
                  pl.BlockSpec((tq, hw), lambda b, h, i: (b * nq + i, h)),
                  pl.BlockSpec((seq, hw), lambda b, h, i: (b, heads + h)),
                  pl.BlockSpec((seq, hw), lambda b, h, i: (b, h))],
        out_specs=pl.BlockSpec((tq, hw), lambda b, h, i: (b * nq + i, h)),
        out_shape=jax.ShapeDtypeStruct((bsz * seq, d), BF16),
        scratch_shapes=[pltpu.VMEM((2, tq, hw), F32)],
        compiler_params=_params(3),
        name="diff_attention",
    )(lambda_qk, g_subln.reshape(1, hw), qk, qk, v)


def _conv_kernel(u_ref, prev_ref, next_ref, b_ref, w_ref, o_ref, *, tiles_per_seq):
    i = pl.program_id(0)
    ts = u_ref.shape[0]
    u = u_ref[...]
    pos = i % tiles_per_seq
    prev_row = jnp.where(pos == 0, 0.0, prev_ref[SUBLANES - 1:SUBLANES, :])
    next_row = jnp.where(pos == tiles_per_seq - 1, 0.0, next_ref[0:1, :])
    row = lax.broadcasted_iota(jnp.int32, u.shape, 0)
    u_prev = jnp.where(row == 0, prev_row, pltpu.roll(u, 1, 0))
    u_next = jnp.where(row == ts - 1, next_row, pltpu.roll(u, ts - 1, 0))
    w = w_ref[...]
    y = u_prev * w[0:1] + u * w[1:2] + u_next * w[2:3]
    o_ref[...] = (b_ref[...] * y).astype(o_ref.dtype)


def _gated_conv(u, b_gate, conv_w, seq):
    m, d = u.shape
    ts = min(512, seq)
    tc = min(1024, d)
    rb = ts // SUBLANES
    n_rb = m // SUBLANES
    kern = functools.partial(_conv_kernel, tiles_per_seq=seq // ts)
    return pl.pallas_call(
        kern,
        grid=(m // ts, d // tc),
        in_specs=[pl.BlockSpec((ts, tc), lambda i, j: (i, j)),
                  pl.BlockSpec((SUBLANES, tc), lambda i, j: (jnp.maximum(i * rb - 1, 0), j)),
                  pl.BlockSpec((SUBLANES, tc), lambda i, j: (jnp.minimum((i + 1) * rb, n_rb - 1), j)),
                  pl.BlockSpec((ts, tc), lambda i, j: (i, j)),
                  pl.BlockSpec((3, tc), lambda i, j: (0, j))],
        out_specs=pl.BlockSpec((ts, tc), lambda i, j: (i, j)),
        out_shape=jax.ShapeDtypeStruct((m, d), BF16),
        compiler_params=_params(2),
        name="gated_conv",
    )(u, u, u, b_gate, conv_w)


def _ffn(x, h, gate, wg, wu, wo, seq):
    m, d = x.shape
    fp = wg.shape[1]
    tm = min(1024, seq)
    tf = min(512, fp)
    hidden, = _matmul(
        [h], [wg, wu], [], [(jax.ShapeDtypeStruct((m, fp), BF16), _tile_spec(tm, tf))],
        _epi_swiglu, m=m, n=fp, kdim=d, tm=tm, tn=tf, tk=d, name="ffn_in")
    tn = min(1024, d)
    tk = min(1024, fp)
    out, = _matmul(
        [hidden], [wo],
        [(x, _tile_spec(tm, tn)), (gate, _seq_vec_spec(tm, tn, seq))],
        [(jax.ShapeDtypeStruct((m, d), F32), _tile_spec(tm, tn))],
        functools.partial(_epi_residual, 0.5), m=m, n=d, kdim=fp, tm=tm, tn=tn, tk=tk,
        name="ffn_out")
    return out


def _rope_tables(seq):
    inv_freq = ROPE_THETA ** (-jnp.arange(0, HEAD_DIM, 2, dtype=F32) / HEAD_DIM)
    ang = jnp.arange(seq, dtype=F32)[:, None] * inv_freq[None, :]
    cos = jnp.cos(ang)
    sin = jnp.sin(ang)
    return jnp.concatenate([cos, cos], axis=-1), jnp.concatenate([-sin, sin], axis=-1)


def _mixer(x, h, gate, w, bsz, seq, lam_init):
    m, d = x.shape
    w_in = w["w_in"]
    tm = min(1024, seq)

    tn = min(1024, d)
    cos, sin = _rope_tables(seq)
    rope_spec = pl.BlockSpec((tm, HEAD_DIM), lambda i, j, k: (i % (seq // tm), 0))
    qk, = _matmul(
        [h], [w_in], [(cos, rope_spec), (sin, rope_spec)],
        [(jax.ShapeDtypeStruct((m, 2 * d), BF16), _tile_spec(tm, tn))],
        functools.partial(_epi_rope, d // tn, 1.0 / math.sqrt(HEAD_DIM), tn),
        m=m, n=2 * d, kdim=d, tm=tm, tn=tn, tk=d, name="proj_qk")
    v, = _matmul(
        [h], [w_in], [], [(jax.ShapeDtypeStruct((m, d), BF16), _tile_spec(tm, tn))],
        _epi_cast, m=m, n=d, kdim=d, tm=tm, tn=tn, tk=d, b_col_off=(2 * d // tn,), name="proj_v")
    gates, = _matmul(
        [h], [w_in], [], [(jax.ShapeDtypeStruct((m, 2 * d), F32), _tile_spec(tm, tn))],
        _epi_sigmoid, m=m, n=2 * d, kdim=d, tm=tm, tn=tn, tk=d, b_col_off=(6 * d // tn,),
        name="proj_gates")
    tmc = min(512, seq)
    tnc = min(512, d)
    nb = d // tnc
    b_gate, u = _matmul(
        [h], [w_in, w_in, w_in], [],
        [(jax.ShapeDtypeStruct((m, d), F32), _tile_spec(tmc, tnc)),
         (jax.ShapeDtypeStruct((m, d), F32), _tile_spec(tmc, tnc))],
        _epi_conv_operands, m=m, n=d, kdim=d, tm=tmc, tn=tnc, tk=d,
        b_col_off=(3 * nb, 4 * nb, 5 * nb), name="proj_conv")

    o = _attention(qk, v, w["lambda_qk"], w["g_subln"], bsz=bsz, seq=seq, d=d, lam_init=lam_init)
    z = _gated_conv(u, b_gate, w["conv_w"], seq)

    tmm = min(512, seq)
    tnm = min(512, d)
    merged, = _matmul(
        [o, z], [w["w_branch_attn"], w["w_branch_conv"]],
        [(gates, _tile_spec(tmm, tnm)), (gates, _tile_spec(tmm, tnm, d // tnm))],
        [(jax.ShapeDtypeStruct((m, d), BF16), _tile_spec(tmm, tnm))],
        _epi_merge, m=m, n=d, kdim=d, tm=tmm, tn=tnm, tk=d, pair_a=(0, 1), name="branch_merge")

    out, = _matmul(
        [merged], [w["w_out"]],
        [(x, _tile_spec(tm, tn)), (gate, _seq_vec_spec(tm, tn, seq))],
        [(jax.ShapeDtypeStruct((m, d), F32), _tile_spec(tm, tn))],
        functools.partial(_epi_residual, 1.0), m=m, n=d, kdim=d, tm=tm, tn=tn, tk=d,
        name="out_proj")
    return out


def _encode(x3, mod, layers, g_final):
    bsz, seq, d = x3.shape
    x = x3.reshape(bsz * seq, d)
    for l, w in enumerate(layers):
        lam_init = 0.8 - 0.6 * math.exp(-0.3 * l)
        md = mod[l].reshape(bsz, N_SUBLAYERS, 3, 1, d)
        shift = [md[:, s, 0] for s in range(N_SUBLAYERS)]
        scale = [md[:, s, 1] for s in range(N_SUBLAYERS)]
        gate = [md[:, s, 2] for s in range(N_SUBLAYERS)]
        h = _norm_mod(x, w["g_norm"][0], scale[0], shift[0], seq)
        x = _ffn(x, h, gate[0], *w["ffn"][0], seq)
        h = _norm_mod(x, w["g_norm"][1], scale[1], shift[1], seq)
        x = _mixer(x, h, gate[1], w, bsz, seq, lam_init)
        h = _norm_mod(x, w["g_norm"][2], scale[2], shift[2], seq)
        x = _ffn(x, h, gate[2], *w["ffn"][1], seq)
    return _final_norm(x, g_final, seq).reshape(bsz, seq, d)


def kernel(x_prompt, x_sample, c_prompt, c_sample, w_mod, b_mod, g_norm, w_ffn_in, w_ffn_out, w_in,
           conv_w, lambda_qk, g_subln, w_branch_attn, w_branch_conv, w_out, g_final):
    depth = w_mod.shape[0]
    ffn_dim = w_ffn_out.shape[2]
    fp = _round_up(ffn_dim, FFN_PAD) if ffn_dim > FFN_PAD else ffn_dim
    pad = fp - ffn_dim
    bp, bs = c_prompt.shape[0], c_sample.shape[0]
    rows = _round_up(bp + bs, SUBLANES)
    c_all = jnp.pad(jnp.concatenate([c_prompt, c_sample], axis=0), ((0, rows - bp - bs), (0, 0)))

    layers, mod_p, mod_s = [], [], []
    for l in range(depth):
        mod = _modulation(c_all, w_mod[l], b_mod[l])
        mod_p.append(mod[:bp])
        mod_s.append(mod[bp:bp + bs])
        ffn = []
        for j in range(2):
            wi = w_ffn_in[l, j].astype(BF16)
            wg = jnp.pad(wi[:, :ffn_dim], ((0, 0), (0, pad)))
            wu = jnp.pad(wi[:, ffn_dim:], ((0, 0), (0, pad)))
            wo = jnp.pad(w_ffn_out[l, j].astype(BF16), ((0, pad), (0, 0)))
            ffn.append((wg, wu, wo))
        layers.append(dict(
            ffn=ffn, g_norm=g_norm[l], w_in=w_in[l].astype(BF16), conv_w=conv_w[l],
            lambda_qk=lambda_qk[l], g_subln=g_subln[l],
            w_branch_attn=w_branch_attn[l].astype(BF16),
            w_branch_conv=w_branch_conv[l].astype(BF16), w_out=w_out[l].astype(BF16)))

    y_prompt = _encode(x_prompt, mod_p, layers, g_final)
    y_sample = _encode(x_sample, mod_s, layers, g_final)
    return (y_prompt, y_sample)
```

```python
import functools
import math

import jax
import jax.numpy as jnp
from jax import lax
from jax.experimental import pallas as pl
from jax.experimental.pallas import tpu as pltpu

HEAD_DIM = 128
NORM_EPS = 1e-6
ROPE_THETA = 10000.0
N_SUBLAYERS = 3
LANES = 128
SUBLANES = 8
VMEM_LIMIT_BYTES = 60 * 1024 * 1024
FFN_PAD = 1024
LOG2_E = 1.4426950408889634

F32 = jnp.float32
BF16 = jnp.bfloat16


def _params(n_grid_dims):
    return pltpu.CompilerParams(
        dimension_semantics=("arbitrary",) * n_grid_dims,
        vmem_limit_bytes=VMEM_LIMIT_BYTES,
    )


def _round_up(x, m):
    return (x + m - 1) // m * m


def _largest_tile(total, cap):
    best = LANES
    for t in range(LANES, min(total, cap) + 1, LANES):
        if total % t == 0:
            best = t
    return best


def _mod_kernel(c_ref, w_ref, b_ref, o_ref):
    o_ref[...] = jnp.dot(c_ref[...].astype(BF16), w_ref[...].astype(BF16),
                         preferred_element_type=F32) + b_ref[...]


def _modulation(c_all, w_mod, b_mod):
    rows, d = c_all.shape
    n = w_mod.shape[1]
    tn = min(512, n)
    return pl.pallas_call(
        _mod_kernel,
        grid=(n // tn,),
        in_specs=[pl.BlockSpec((rows, d), lambda j: (0, 0)),
                  pl.BlockSpec((d, tn), lambda j: (0, j)),
                  pl.BlockSpec((1, tn), lambda j: (0, j))],
        out_specs=pl.BlockSpec((rows, tn), lambda j: (0, j)),
        out_shape=jax.ShapeDtypeStruct((rows, n), F32),
        compiler_params=_params(1),
        name="modulation",
    )(c_all, w_mod, b_mod.reshape(1, n))


def _norm_mod_kernel(x_ref, g_ref, scale_ref, shift_ref, o_ref):
    x = x_ref[...]
    ms = jnp.mean(x * x, axis=-1, keepdims=True)
    xn = (x * lax.rsqrt(ms + NORM_EPS)) * g_ref[...]
    o_ref[...] = (xn * (1.0 + scale_ref[...]) + shift_ref[...]).astype(o_ref.dtype)


def _norm_kernel(x_ref, g_ref, o_ref):
    x = x_ref[...]
    ms = jnp.mean(x * x, axis=-1, keepdims=True)
    o_ref[...] = ((x * lax.rsqrt(ms + NORM_EPS)) * g_ref[...]).astype(o_ref.dtype)


def _norm_mod(x, g, scale, shift, seq):
    m, d = x.shape
    tm = min(512, seq)
    vec = pl.BlockSpec((None, 1, d), lambda i: (i * tm // seq, 0, 0))
    return pl.pallas_call(
        _norm_mod_kernel,
        grid=(m // tm,),
        in_specs=[pl.BlockSpec((tm, d), lambda i: (i, 0)),
                  pl.BlockSpec((1, d), lambda i: (0, 0)), vec, vec],
        out_specs=pl.BlockSpec((tm, d), lambda i: (i, 0)),
        out_shape=jax.ShapeDtypeStruct((m, d), BF16),
        compiler_params=_params(1),
        name="norm_mod",
    )(x, g.reshape(1, d), scale, shift)


def _final_norm(x, g, seq):
    m, d = x.shape
    tm = min(512, seq)
    return pl.pallas_call(
        _norm_kernel,
        grid=(m // tm,),
        in_specs=[pl.BlockSpec((tm, d), lambda i: (i, 0)),
                  pl.BlockSpec((1, d), lambda i: (0, 0))],
        out_specs=pl.BlockSpec((tm, d), lambda i: (i, 0)),
        out_shape=jax.ShapeDtypeStruct((m, d), F32),
        compiler_params=_params(1),
        name="final_norm",
    )(x, g.reshape(1, d))


def _mm_body(*refs, n_a, n_b, n_e, n_o, nk, pair_a, epilogue):
    a = refs[:n_a]
    b = refs[n_a:n_a + n_b]
    e = refs[n_a + n_b:n_a + n_b + n_e]
    o = refs[n_a + n_b + n_e:n_a + n_b + n_e + n_o]
    acc = refs[n_a + n_b + n_e + n_o:]
    ids = (pl.program_id(0), pl.program_id(1))
    r = [jnp.dot(a[pair_a[t]][...], b[t][...], preferred_element_type=F32) for t in range(n_b)]
    if nk == 1:
        epilogue(ids, r, e, o)
        return
    k = pl.program_id(2)

    @pl.when(k == 0)
    def _():
        for t in range(n_b):
            acc[t][...] = r[t]

    @pl.when(jnp.logical_and(k > 0, k < nk - 1))
    def _():
        for t in range(n_b):
            acc[t][...] += r[t]

    @pl.when(k == nk - 1)
    def _():
        epilogue(ids, [acc[t][...] + r[t] for t in range(n_b)], e, o)


def _matmul(a_arrs, b_arrs, extras, outs, epilogue, *, m, n, kdim, tm, tn, tk,
            pair_a=None, b_col_off=None, name):
    n_b = len(b_arrs)
    pair_a = tuple(pair_a) if pair_a is not None else (0,) * n_b
    b_col_off = tuple(b_col_off) if b_col_off is not None else (0,) * n_b
    assert m % tm == 0 and n % tn == 0 and kdim % tk == 0
    nk = kdim // tk
    in_specs = [pl.BlockSpec((tm, tk), lambda i, j, k: (i, k)) for _ in a_arrs]
    for off in b_col_off:
        in_specs.append(pl.BlockSpec((tk, tn), functools.partial(
            lambda i, j, k, off: (k, j + off), off=off)))
    in_specs += [s for _, s in extras]
    scratch = [pltpu.VMEM((tm, tn), F32) for _ in range(n_b)] if nk > 1 else []
    body = functools.partial(_mm_body, n_a=len(a_arrs), n_b=n_b, n_e=len(extras), n_o=len(outs),
                             nk=nk, pair_a=pair_a, epilogue=epilogue)
    res = pl.pallas_call(
        body,
        grid=(m // tm, n // tn, nk),
        in_specs=in_specs,
        out_specs=[s for _, s in outs],
        out_shape=[sh for sh, _ in outs],
        scratch_shapes=scratch,
        compiler_params=_params(3),
        name=name,
    )(*a_arrs, *b_arrs, *[x for x, _ in extras])
    return res


def _tile_spec(tm, tn, col_off=0):
    return pl.BlockSpec((tm, tn), lambda i, j, k: (i, j + col_off))


def _seq_vec_spec(tm, tn, seq):
    return pl.BlockSpec((None, 1, tn), lambda i, j, k: (i * tm // seq, 0, j))


def _epi_swiglu(ids, r, e, o):
    g, u = r
    o[0][...] = ((g * jax.nn.sigmoid(g)) * u).astype(o[0].dtype)


def _epi_residual(coef, ids, r, e, o):
    x_ref, gate_ref = e
    gate = gate_ref[...] if coef == 1.0 else coef * gate_ref[...]
    o[0][...] = x_ref[...] + gate * r[0]


def _epi_rope(n_q_tiles, q_scale, tn, ids, r, e, o):
    cos = e[0][...]
    sin = e[1][...]
    s = jnp.where(ids[1] < n_q_tiles, q_scale, 1.0).astype(F32)
    acc = r[0]
    for g in range(tn // HEAD_DIM):
        blk = acc[:, g * HEAD_DIM:(g + 1) * HEAD_DIM]
        rot = pltpu.roll(blk, HEAD_DIM // 2, 1)
        o[0][:, g * HEAD_DIM:(g + 1) * HEAD_DIM] = ((blk * cos + rot * sin) * s).astype(o[0].dtype)


def _epi_cast(ids, r, e, o):
    o[0][...] = r[0].astype(o[0].dtype)


def _epi_conv_operands(ids, r, e, o):
    o[0][...] = r[0]
    o[1][...] = r[1] * r[2]


def _epi_sigmoid(ids, r, e, o):
    o[0][...] = jax.nn.sigmoid(r[0])


def _epi_merge(ids, r, e, o):
    o[0][...] = (e[0][...] * r[0] + e[1][...] * r[1]).astype(o[0].dtype)


def _attn_kernel(lq_ref, g_ref, q_ref, k_ref, v_ref, o_ref, vt_ref, acc_ref, sa_ref, sb_ref, *,
                 tk, lam_init):
    tq = q_ref.shape[0]
    seq = k_ref.shape[0]
    nk = seq // tk

    @pl.when(pl.program_id(2) == 0)
    def _():
        def tr(j, carry):
            start = pl.multiple_of(j * tk, tk)
            vt_ref[j] = v_ref[pl.ds(start, tk), :].astype(F32).T.astype(BF16)
            return carry
        lax.fori_loop(0, nk, tr, 0)

    q = (q_ref[:, :HEAD_DIM], q_ref[:, HEAD_DIM:])
    acc_ref[...] = jnp.zeros_like(acc_ref)

    def scores(j, s_ref):
        start = pl.multiple_of(j * tk, tk)
        kblk = k_ref[pl.ds(start, tk), :]
        cmax = []
        for c in range(2):
            st = lax.dot_general(kblk[:, c * HEAD_DIM:(c + 1) * HEAD_DIM], q[c],
                                 (((1,), (1,)), ((), ())), preferred_element_type=F32)
            s_ref[c] = st
            cmax.append(jnp.max(st, axis=0, keepdims=True))
        return cmax

    def consume(j, s_ref, cmax, stats):
        vt = vt_ref[j]
        out = []
        for c in range(2):
            m, l = stats[c]
            m_new = jnp.maximum(m, cmax[c])
            alpha = jnp.exp2(m - m_new)
            p = jnp.exp2(s_ref[c] - m_new)
            l_new = alpha * l + jnp.sum(p, axis=0, keepdims=True)
            acc_ref[c] = acc_ref[c] * alpha + jnp.dot(vt, p.astype(BF16), preferred_element_type=F32)
            out.append((m_new, l_new))
        return out

    def body(jj, carry):
        stats, cmax_a = carry
        ja = 2 * jj
        cmax_b = scores(ja + 1, sb_ref)
        stats = consume(ja, sa_ref, cmax_a, stats)
        cmax_a = scores(jnp.minimum(ja + 2, nk - 1), sa_ref)
        stats = consume(ja + 1, sb_ref, cmax_b, stats)
        return stats, cmax_a

    neg = jnp.full((1, tq), -jnp.inf, F32)
    zero = jnp.zeros((1, tq), F32)
    cmax0 = scores(0, sa_ref)
    stats, _ = lax.fori_loop(0, nk // 2, body, ([(neg, zero), (neg, zero)], cmax0))
    l0, l1 = stats[0][1], stats[1][1]

    lq = lq_ref[...]
    lam = (jnp.exp(jnp.sum(lq[0:1] * lq[1:2], axis=-1, keepdims=True))
           - jnp.exp(jnp.sum(lq[2:3] * lq[3:4], axis=-1, keepdims=True)) + lam_init)
    ot = acc_ref[0] / l0 - lam * (acc_ref[1] / l1)
    ms = jnp.mean(ot * ot, axis=0, keepdims=True)
    o = (ot * lax.rsqrt(ms + NORM_EPS)).T * g_ref[...]
    o_ref[...] = (o * (1.0 - lam_init)).astype(o_ref.dtype)


def _attention(qk, v, lambda_qk, g_subln, *, bsz, seq, d, lam_init):
    heads = d // (2 * HEAD_DIM)
    hw = 2 * HEAD_DIM
    tq = min(512, seq)
    tk = min(1024, seq // 2)
    assert seq % (2 * tk) == 0
    nq = seq // tq
    kern = functools.partial(_attn_kernel, tk=tk, lam_init=lam_init)
    return pl.pallas_call(
        kern,
        grid=(bsz, heads, nq),
        in_specs=[pl.BlockSpec((4, HEAD_DIM), lambda b, h, i: (0, 0)),
                  pl.BlockSpec((1, hw), lambda b, h, i: (0, 0)),
                  pl.BlockSpec((tq, hw), lambda b, h, i: (b * nq + i, h)),
                  pl.BlockSpec((seq, hw), lambda b, h, i: (b, heads + h)),
                  pl.BlockSpec((seq, hw), lambda b, h, i: (b, h))],
        out_specs=pl.BlockSpec((tq, hw), lambda b, h, i: (b * nq + i, h)),
        out_shape=jax.ShapeDtypeStruct((bsz * seq, d), BF16),
        scratch_shapes=[pltpu.VMEM((seq // tk, hw, tk), BF16), pltpu.VMEM((2, hw, tq), F32),
                        pltpu.VMEM((2, tk, tq), F32), pltpu.VMEM((2, tk, tq), F32)],
        compiler_params=_params(3),
        name="diff_attention",
    )(lambda_qk, g_subln.reshape(1, hw), qk, qk, v)


def _conv_kernel(u_ref, prev_ref, next_ref, b_ref, w_ref, o_ref, *, tiles_per_seq):
    i = pl.program_id(0)
    ts = u_ref.shape[0]
    u = u_ref[...]
    pos = i % tiles_per_seq
    prev_row = jnp.where(pos == 0, 0.0, prev_ref[SUBLANES - 1:SUBLANES, :])
    next_row = jnp.where(pos == tiles_per_seq - 1, 0.0, next_ref[0:1, :])
    row = lax.broadcasted_iota(jnp.int32, u.shape, 0)
    u_prev = jnp.where(row == 0, prev_row, pltpu.roll(u, 1, 0))
    u_next = jnp.where(row == ts - 1, next_row, pltpu.roll(u, ts - 1, 0))
    w = w_ref[...]
    y = u_prev * w[0:1] + u * w[1:2] + u_next * w[2:3]
    o_ref[...] = (b_ref[...] * y).astype(o_ref.dtype)


def _gated_conv(u, b_gate, conv_w, seq):
    m, d = u.shape
    ts = min(512, seq)
    tc = min(1024, d)
    rb = ts // SUBLANES
    n_rb = m // SUBLANES
    kern = functools.partial(_conv_kernel, tiles_per_seq=seq // ts)
    return pl.pallas_call(
        kern,
        grid=(m // ts, d // tc),
        in_specs=[pl.BlockSpec((ts, tc), lambda i, j: (i, j)),
                  pl.BlockSpec((SUBLANES, tc), lambda i, j: (jnp.maximum(i * rb - 1, 0), j)),
                  pl.BlockSpec((SUBLANES, tc), lambda i, j: (jnp.minimum((i + 1) * rb, n_rb - 1), j)),
                  pl.BlockSpec((ts, tc), lambda i, j: (i, j)),
                  pl.BlockSpec((3, tc), lambda i, j: (0, j))],
        out_specs=pl.BlockSpec((ts, tc), lambda i, j: (i, j)),
        out_shape=jax.ShapeDtypeStruct((m, d), BF16),
        compiler_params=_params(2),
        name="gated_conv",
    )(u, u, u, b_gate, conv_w)


def _ffn(x, h, gate, wg, wu, wo, seq):
    m, d = x.shape
    fp = wg.shape[1]
    tm = min(1024, seq)
    tf = min(512, fp)
    hidden, = _matmul(
        [h], [wg, wu], [], [(jax.ShapeDtypeStruct((m, fp), BF16), _tile_spec(tm, tf))],
        _epi_swiglu, m=m, n=fp, kdim=d, tm=tm, tn=tf, tk=d, name="ffn_in")
    tn = min(1024, d)
    tk = _largest_tile(fp, 3072)
    out, = _matmul(
        [hidden], [wo],
        [(x, _tile_spec(tm, tn)), (gate, _seq_vec_spec(tm, tn, seq))],
        [(jax.ShapeDtypeStruct((m, d), F32), _tile_spec(tm, tn))],
        functools.partial(_epi_residual, 0.5), m=m, n=d, kdim=fp, tm=tm, tn=tn, tk=tk,
        name="ffn_out")
    return out


def _rope_tables(seq):
    inv_freq = ROPE_THETA ** (-jnp.arange(0, HEAD_DIM, 2, dtype=F32) / HEAD_DIM)
    ang = jnp.arange(seq, dtype=F32)[:, None] * inv_freq[None, :]
    cos = jnp.cos(ang)
    sin = jnp.sin(ang)
    return jnp.concatenate([cos, cos], axis=-1), jnp.concatenate([-sin, sin], axis=-1)


def _mixer(x, h, gate, w, bsz, seq, lam_init):
    m, d = x.shape
    w_in = w["w_in"]
    tm = min(1024, seq)

    tn = min(1024, d)
    cos, sin = _rope_tables(seq)
    rope_spec = pl.BlockSpec((tm, HEAD_DIM), lambda i, j, k: (i % (seq // tm), 0))
    qk, = _matmul(
        [h], [w_in], [(cos, rope_spec), (sin, rope_spec)],
        [(jax.ShapeDtypeStruct((m, 2 * d), BF16), _tile_spec(tm, tn))],
        functools.partial(_epi_rope, d // tn, LOG2_E / math.sqrt(HEAD_DIM), tn),
        m=m, n=2 * d, kdim=d, tm=tm, tn=tn, tk=d, name="proj_qk")
    v, = _matmul(
        [h], [w_in], [], [(jax.ShapeDtypeStruct((m, d), BF16), _tile_spec(tm, tn))],
        _epi_cast, m=m, n=d, kdim=d, tm=tm, tn=tn, tk=d, b_col_off=(2 * d // tn,), name="proj_v")
    gates, = _matmul(
        [h], [w_in], [], [(jax.ShapeDtypeStruct((m, 2 * d), F32), _tile_spec(tm, tn))],
        _epi_sigmoid, m=m, n=2 * d, kdim=d, tm=tm, tn=tn, tk=d, b_col_off=(6 * d // tn,),
        name="proj_gates")
    tmc = min(512, seq)
    tnc = min(512, d)
    nb = d // tnc
    b_gate, u = _matmul(
        [h], [w_in, w_in, w_in], [],
        [(jax.ShapeDtypeStruct((m, d), F32), _tile_spec(tmc, tnc)),
         (jax.ShapeDtypeStruct((m, d), F32), _tile_spec(tmc, tnc))],
        _epi_conv_operands, m=m, n=d, kdim=d, tm=tmc, tn=tnc, tk=d,
        b_col_off=(3 * nb, 4 * nb, 5 * nb), name="proj_conv")

    o = _attention(qk, v, w["lambda_qk"], w["g_subln"], bsz=bsz, seq=seq, d=d, lam_init=lam_init)
    z = _gated_conv(u, b_gate, w["conv_w"], seq)

    tmm = min(512, seq)
    tnm = min(512, d)
    merged, = _matmul(
        [o, z], [w["w_branch_attn"], w["w_branch_conv"]],
        [(gates, _tile_spec(tmm, tnm)), (gates, _tile_spec(tmm, tnm, d // tnm))],
        [(jax.ShapeDtypeStruct((m, d), BF16), _tile_spec(tmm, tnm))],
        _epi_merge, m=m, n=d, kdim=d, tm=tmm, tn=tnm, tk=d, pair_a=(0, 1), name="branch_merge")

    out, = _matmul(
        [merged], [w["w_out"]],
        [(x, _tile_spec(tm, tn)), (gate, _seq_vec_spec(tm, tn, seq))],
        [(jax.ShapeDtypeStruct((m, d), F32), _tile_spec(tm, tn))],
        functools.partial(_epi_residual, 1.0), m=m, n=d, kdim=d, tm=tm, tn=tn, tk=d,
        name="out_proj")
    return out


def _encode(x3, mod, layers, g_final):
    bsz, seq, d = x3.shape
    x = x3.reshape(bsz * seq, d)
    for l, w in enumerate(layers):
        lam_init = 0.8 - 0.6 * math.exp(-0.3 * l)
        md = mod[l].reshape(bsz, N_SUBLAYERS, 3, 1, d)
        shift = [md[:, s, 0] for s in range(N_SUBLAYERS)]
        scale = [md[:, s, 1] for s in range(N_SUBLAYERS)]
        gate = [md[:, s, 2] for s in range(N_SUBLAYERS)]
        h = _norm_mod(x, w["g_norm"][0], scale[0], shift[0], seq)
        x = _ffn(x, h, gate[0], *w["ffn"][0], seq)
        h = _norm_mod(x, w["g_norm"][1], scale[1], shift[1], seq)
        x = _mixer(x, h, gate[1], w, bsz, seq, lam_init)
        h = _norm_mod(x, w["g_norm"][2], scale[2], shift[2], seq)
        x = _ffn(x, h, gate[2], *w["ffn"][1], seq)
    return _final_norm(x, g_final, seq).reshape(bsz, seq, d)


def kernel(x_prompt, x_sample, c_prompt, c_sample, w_mod, b_mod, g_norm, w_ffn_in, w_ffn_out, w_in,
           conv_w, lambda_qk, g_subln, w_branch_attn, w_branch_conv, w_out, g_final):
    depth = w_mod.shape[0]
    ffn_dim = w_ffn_out.shape[2]
    fp = _round_up(ffn_dim, FFN_PAD) if ffn_dim > FFN_PAD else ffn_dim
    pad = fp - ffn_dim
    bp, bs = c_prompt.shape[0], c_sample.shape[0]
    rows = _round_up(bp + bs, SUBLANES)
    c_all = jnp.pad(jnp.concatenate([c_prompt, c_sample], axis=0), ((0, rows - bp - bs), (0, 0)))

    layers, mod_p, mod_s = [], [], []
    for l in range(depth):
        mod = _modulation(c_all, w_mod[l], b_mod[l])
        mod_p.append(mod[:bp])
        mod_s.append(mod[bp:bp + bs])
        ffn = []
        for j in range(2):
            wg = jnp.pad(w_ffn_in[l, j, :, :ffn_dim].astype(BF16), ((0, 0), (0, pad)))
            wu = jnp.pad(w_ffn_in[l, j, :, ffn_dim:].astype(BF16), ((0, 0), (0, pad)))
            wo = jnp.pad(w_ffn_out[l, j].astype(BF16), ((0, pad), (0, 0)))
            ffn.append((wg, wu, wo))
        layers.append(dict(
            ffn=ffn, g_norm=g_norm[l], w_in=w_in[l].astype(BF16), conv_w=conv_w[l],
            lambda_qk=lambda_qk[l], g_subln=g_subln[l],
            w_branch_attn=w_branch_attn[l].astype(BF16),
            w_branch_conv=w_branch_conv[l].astype(BF16), w_out=w_out[l].astype(BF16)))

    y_prompt = _encode(x_prompt, mod_p, layers, g_final)
    y_sample = _encode(x_sample, mod_s, layers, g_final)
    return (y_prompt, y_sample)
```

```python
import functools
import math

import jax
import jax.numpy as jnp
from jax import lax
from jax.experimental import pallas as pl
from jax.experimental.pallas import tpu as pltpu

HEAD_DIM = 128
NORM_EPS = 1e-6
ROPE_THETA = 10000.0
N_SUBLAYERS = 3
LANES = 128
SUBLANES = 8
VMEM_LIMIT_BYTES = 60 * 1024 * 1024
FFN_PAD = 1024
LOG2_E = 1.4426950408889634

F32 = jnp.float32
BF16 = jnp.bfloat16


def _params(n_grid_dims, flags=None):
    return pltpu.CompilerParams(
        dimension_semantics=("arbitrary",) * n_grid_dims,
        vmem_limit_bytes=VMEM_LIMIT_BYTES,
        flags=flags,
    )


def _round_up(x, m):
    return (x + m - 1) // m * m


def _largest_tile(total, cap):
    best = LANES
    for t in range(LANES, min(total, cap) + 1, LANES):
        if total % t == 0:
            best = t
    return best


def _cast_pad_cols_kernel(x_ref, o_ref):
    n = x_ref.shape[1]
    o_ref[:, :n] = x_ref[...].astype(o_ref.dtype)
    if o_ref.shape[1] > n:
        o_ref[:, n:] = jnp.zeros((o_ref.shape[0], o_ref.shape[1] - n), o_ref.dtype)


def _cast_pad_rows_kernel(x_ref, o_ref):
    n = x_ref.shape[0]
    o_ref[:n, :] = x_ref[...].astype(o_ref.dtype)
    if o_ref.shape[0] > n:
        o_ref[n:, :] = jnp.zeros((o_ref.shape[0] - n, o_ref.shape[1]), o_ref.dtype)


def _ffn_in_half(w, l, j, half, f, fp):
    d = w.shape[2]
    tr = min(256, d)
    return pl.pallas_call(
        _cast_pad_cols_kernel,
        grid=(d // tr,),
        in_specs=[pl.BlockSpec((None, None, tr, f), lambda i: (l, j, i, half))],
        out_specs=pl.BlockSpec((tr, fp), lambda i: (i, 0)),
        out_shape=jax.ShapeDtypeStruct((d, fp), BF16),
        compiler_params=_params(1),
        name="cast_ffn_in",
    )(w)


def _ffn_out_padded(w, l, j, fp):
    f, d = w.shape[2:]
    tc = min(256, d)
    return pl.pallas_call(
        _cast_pad_rows_kernel,
        grid=(d // tc,),
        in_specs=[pl.BlockSpec((None, None, f, tc), lambda c: (l, j, 0, c))],
        out_specs=pl.BlockSpec((fp, tc), lambda c: (0, c)),
        out_shape=jax.ShapeDtypeStruct((fp, d), BF16),
        compiler_params=_params(1),
        name="cast_ffn_out",
    )(w)


def _mod_kernel(c_ref, w_ref, b_ref, o_ref):
    o_ref[...] = jnp.dot(c_ref[...].astype(BF16), w_ref[...].astype(BF16),
                         preferred_element_type=F32) + b_ref[...]


def _modulation(c_all, w_mod, b_mod):
    rows, d = c_all.shape
    n = w_mod.shape[1]
    tn = min(512, n)
    return pl.pallas_call(
        _mod_kernel,
        grid=(n // tn,),
        in_specs=[pl.BlockSpec((rows, d), lambda j: (0, 0)),
                  pl.BlockSpec((d, tn), lambda j: (0, j)),
                  pl.BlockSpec((1, tn), lambda j: (0, j))],
        out_specs=pl.BlockSpec((rows, tn), lambda j: (0, j)),
        out_shape=jax.ShapeDtypeStruct((rows, n), F32),
        compiler_params=_params(1),
        name="modulation",
    )(c_all, w_mod, b_mod.reshape(1, n))


def _norm_mod_kernel(x_ref, g_ref, scale_ref, shift_ref, o_ref):
    x = x_ref[...]
    ms = jnp.mean(x * x, axis=-1, keepdims=True)
    xn = (x * lax.rsqrt(ms + NORM_EPS)) * g_ref[...]
    o_ref[...] = (xn * (1.0 + scale_ref[...]) + shift_ref[...]).astype(o_ref.dtype)


def _norm_kernel(x_ref, g_ref, o_ref):
    x = x_ref[...]
    ms = jnp.mean(x * x, axis=-1, keepdims=True)
    o_ref[...] = ((x * lax.rsqrt(ms + NORM_EPS)) * g_ref[...]).astype(o_ref.dtype)


def _norm_mod(x, g, scale, shift, seq):
    m, d = x.shape
    tm = min(512, seq)
    vec = pl.BlockSpec((None, 1, d), lambda i: (i * tm // seq, 0, 0))
    return pl.pallas_call(
        _norm_mod_kernel,
        grid=(m // tm,),
        in_specs=[pl.BlockSpec((tm, d), lambda i: (i, 0)),
                  pl.BlockSpec((1, d), lambda i: (0, 0)), vec, vec],
        out_specs=pl.BlockSpec((tm, d), lambda i: (i, 0)),
        out_shape=jax.ShapeDtypeStruct((m, d), BF16),
        compiler_params=_params(1),
        name="norm_mod",
    )(x, g.reshape(1, d), scale, shift)


def _final_norm(x, g, seq):
    m, d = x.shape
    tm = min(512, seq)
    return pl.pallas_call(
        _norm_kernel,
        grid=(m // tm,),
        in_specs=[pl.BlockSpec((tm, d), lambda i: (i, 0)),
                  pl.BlockSpec((1, d), lambda i: (0, 0))],
        out_specs=pl.BlockSpec((tm, d), lambda i: (i, 0)),
        out_shape=jax.ShapeDtypeStruct((m, d), F32),
        compiler_params=_params(1),
        name="final_norm",
    )(x, g.reshape(1, d))


def _mm_body(*refs, n_a, n_b, n_e, n_o, nk, pair_a, epilogue):
    a = refs[:n_a]
    b = refs[n_a:n_a + n_b]
    e = refs[n_a + n_b:n_a + n_b + n_e]
    o = refs[n_a + n_b + n_e:n_a + n_b + n_e + n_o]
    acc = refs[n_a + n_b + n_e + n_o:]
    ids = (pl.program_id(0), pl.program_id(1))
    r = [jnp.dot(a[pair_a[t]][...], b[t][...], preferred_element_type=F32) for t in range(n_b)]
    if nk == 1:
        epilogue(ids, r, e, o)
        return
    k = pl.program_id(2)

    @pl.when(k == 0)
    def _():
        for t in range(n_b):
            acc[t][...] = r[t]

    @pl.when(jnp.logical_and(k > 0, k < nk - 1))
    def _():
        for t in range(n_b):
            acc[t][...] += r[t]

    @pl.when(k == nk - 1)
    def _():
        epilogue(ids, [acc[t][...] + r[t] for t in range(n_b)], e, o)


def _matmul(a_arrs, b_arrs, extras, outs, epilogue, *, m, n, kdim, tm, tn, tk,
            pair_a=None, b_col_off=None, name):
    n_b = len(b_arrs)
    pair_a = tuple(pair_a) if pair_a is not None else (0,) * n_b
    b_col_off = tuple(b_col_off) if b_col_off is not None else (0,) * n_b
    assert m % tm == 0 and n % tn == 0 and kdim % tk == 0
    nk = kdim // tk
    in_specs = [pl.BlockSpec((tm, tk), lambda i, j, k: (i, k)) for _ in a_arrs]
    for off in b_col_off:
        in_specs.append(pl.BlockSpec((tk, tn), functools.partial(
            lambda i, j, k, off: (k, j + off), off=off)))
    in_specs += [s for _, s in extras]
    scratch = [pltpu.VMEM((tm, tn), F32) for _ in range(n_b)] if nk > 1 else []
    body = functools.partial(_mm_body, n_a=len(a_arrs), n_b=n_b, n_e=len(extras), n_o=len(outs),
                             nk=nk, pair_a=pair_a, epilogue=epilogue)
    res = pl.pallas_call(
        body,
        grid=(m // tm, n // tn, nk),
        in_specs=in_specs,
        out_specs=[s for _, s in outs],
        out_shape=[sh for sh, _ in outs],
        scratch_shapes=scratch,
        compiler_params=_params(3),
        name=name,
    )(*a_arrs, *b_arrs, *[x for x, _ in extras])
    return res


def _tile_spec(tm, tn, col_off=0):
    return pl.BlockSpec((tm, tn), lambda i, j, k: (i, j + col_off))


def _seq_vec_spec(tm, tn, seq):
    return pl.BlockSpec((None, 1, tn), lambda i, j, k: (i * tm // seq, 0, j))


def _epi_swiglu(ids, r, e, o):
    g, u = r
    o[0][...] = ((g * jax.nn.sigmoid(g)) * u).astype(o[0].dtype)


def _epi_residual(coef, ids, r, e, o):
    x_ref, gate_ref = e
    gate = gate_ref[...] if coef == 1.0 else coef * gate_ref[...]
    o[0][...] = x_ref[...] + gate * r[0]


def _epi_rope(n_q_tiles, q_scale, tn, ids, r, e, o):
    cos = e[0][...]
    sin = e[1][...]
    s = jnp.where(ids[1] < n_q_tiles, q_scale, 1.0).astype(F32)
    acc = r[0]
    for g in range(tn // HEAD_DIM):
        blk = acc[:, g * HEAD_DIM:(g + 1) * HEAD_DIM]
        rot = pltpu.roll(blk, HEAD_DIM // 2, 1)
        o[0][:, g * HEAD_DIM:(g + 1) * HEAD_DIM] = ((blk * cos + rot * sin) * s).astype(o[0].dtype)


def _epi_cast(ids, r, e, o):
    o[0][...] = r[0].astype(o[0].dtype)


def _epi_conv_operands(ids, r, e, o):
    o[0][...] = r[0]
    o[1][...] = r[1] * r[2]


def _epi_sigmoid(ids, r, e, o):
    o[0][...] = jax.nn.sigmoid(r[0])


def _epi_merge(ids, r, e, o):
    o[0][...] = (e[0][...] * r[0] + e[1][...] * r[1]).astype(o[0].dtype)


def _attn_kernel(lq_ref, g_ref, q_ref, k_ref, v_ref, o_ref, vt_ref, acc_ref, sa_ref, sb_ref, *,
                 tk, lam_init):
    tq = q_ref.shape[0]
    seq = k_ref.shape[0]
    nk = seq // tk

    @pl.when(pl.program_id(2) == 0)
    def _():
        def tr(j, carry):
            start = pl.multiple_of(j * tk, tk)
            vt_ref[j] = v_ref[pl.ds(start, tk), :].astype(F32).T.astype(BF16)
            return carry
        lax.fori_loop(0, nk, tr, 0)

    q = (q_ref[:, :HEAD_DIM], q_ref[:, HEAD_DIM:])
    acc_ref[...] = jnp.zeros_like(acc_ref)

    def scores(j, s_ref):
        start = pl.multiple_of(j * tk, tk)
        kblk = k_ref[pl.ds(start, tk), :]
        cmax = []
        for c in range(2):
            st = lax.dot_general(kblk[:, c * HEAD_DIM:(c + 1) * HEAD_DIM], q[c],
                                 (((1,), (1,)), ((), ())), preferred_element_type=F32)
            s_ref[c] = st
            cmax.append(jnp.max(st, axis=0, keepdims=True))
        return cmax

    def consume(j, s_ref, cmax, stats):
        vt = vt_ref[j]
        out = []
        for c in range(2):
            m, l = stats[c]
            m_new = jnp.maximum(m, cmax[c])
            alpha = jnp.exp2(m - m_new)
            p = jnp.exp2(s_ref[c] - m_new)
            l_new = alpha * l + jnp.sum(p, axis=0, keepdims=True)
            acc_ref[c] = acc_ref[c] * alpha + jnp.dot(vt, p.astype(BF16), preferred_element_type=F32)
            out.append((m_new, l_new))
        return out

    def body(jj, carry):
        stats, cmax_a = carry
        ja = 2 * jj
        cmax_b = scores(ja + 1, sb_ref)
        stats = consume(ja, sa_ref, cmax_a, stats)
        cmax_a = scores(ja + 2, sa_ref)
        stats = consume(ja + 1, sb_ref, cmax_b, stats)
        return stats, cmax_a

    neg = jnp.full((1, tq), -jnp.inf, F32)
    zero = jnp.zeros((1, tq), F32)
    cmax_a = scores(0, sa_ref)
    stats, cmax_a = lax.fori_loop(0, nk // 2 - 1, body, ([(neg, zero), (neg, zero)], cmax_a))
    cmax_b = scores(nk - 1, sb_ref)
    stats = consume(nk - 2, sa_ref, cmax_a, stats)
    stats = consume(nk - 1, sb_ref, cmax_b, stats)
    l0, l1 = stats[0][1], stats[1][1]

    lq = lq_ref[...]
    lam = (jnp.exp(jnp.sum(lq[0:1] * lq[1:2], axis=-1, keepdims=True))
           - jnp.exp(jnp.sum(lq[2:3] * lq[3:4], axis=-1, keepdims=True)) + lam_init)
    ot = acc_ref[0] / l0 - lam * (acc_ref[1] / l1)
    ms = jnp.mean(ot * ot, axis=0, keepdims=True)
    o = (ot * lax.rsqrt(ms + NORM_EPS)).T * g_ref[...]
    o_ref[...] = (o * (1.0 - lam_init)).astype(o_ref.dtype)


def _attention(qk, v, lambda_qk, g_subln, *, bsz, seq, d, lam_init):
    heads = d // (2 * HEAD_DIM)
    hw = 2 * HEAD_DIM
    tq = min(512, seq)
    tk = min(1024, seq // 2)
    assert seq % (2 * tk) == 0
    nq = seq // tq
    kern = functools.partial(_attn_kernel, tk=tk, lam_init=lam_init)
    return pl.pallas_call(
        kern,
        grid=(bsz, heads, nq),
        in_specs=[pl.BlockSpec((4, HEAD_DIM), lambda b, h, i: (0, 0)),
                  pl.BlockSpec((1, hw), lambda b, h, i: (0, 0)),
                  pl.BlockSpec((tq, hw), lambda b, h, i: (b * nq + i, h)),
                  pl.BlockSpec((seq, hw), lambda b, h, i: (b, heads + h)),
                  pl.BlockSpec((seq, hw), lambda b, h, i: (b, h))],
        out_specs=pl.BlockSpec((tq, hw), lambda b, h, i: (b * nq + i, h)),
        out_shape=jax.ShapeDtypeStruct((bsz * seq, d), BF16),
        scratch_shapes=[pltpu.VMEM((seq // tk, hw, tk), BF16), pltpu.VMEM((2, hw, tq), F32),
                        pltpu.VMEM((2, tk, tq), F32), pltpu.VMEM((2, tk, tq), F32)],
        compiler_params=_params(3),
        name="diff_attention",
    )(lambda_qk, g_subln.reshape(1, hw), qk, qk, v)


def _conv_kernel(u_ref, prev_ref, next_ref, b_ref, w_ref, o_ref, *, tiles_per_seq):
    i = pl.program_id(0)
    ts = u_ref.shape[0]
    u = u_ref[...]
    pos = i % tiles_per_seq
    prev_row = jnp.where(pos == 0, 0.0, prev_ref[SUBLANES - 1:SUBLANES, :])
    next_row = jnp.where(pos == tiles_per_seq - 1, 0.0, next_ref[0:1, :])
    row = lax.broadcasted_iota(jnp.int32, u.shape, 0)
    u_prev = jnp.where(row == 0, prev_row, pltpu.roll(u, 1, 0))
    u_next = jnp.where(row == ts - 1, next_row, pltpu.roll(u, ts - 1, 0))
    w = w_ref[...]
    y = u_prev * w[0:1] + u * w[1:2] + u_next * w[2:3]
    o_ref[...] = (b_ref[...] * y).astype(o_ref.dtype)


def _gated_conv(u, b_gate, conv_w, seq):
    m, d = u.shape
    ts = min(512, seq)
    tc = min(1024, d)
    rb = ts // SUBLANES
    n_rb = m // SUBLANES
    kern = functools.partial(_conv_kernel, tiles_per_seq=seq // ts)
    return pl.pallas_call(
        kern,
        grid=(m // ts, d // tc),
        in_specs=[pl.BlockSpec((ts, tc), lambda i, j: (i, j)),
                  pl.BlockSpec((SUBLANES, tc), lambda i, j: (jnp.maximum(i * rb - 1, 0), j)),
                  pl.BlockSpec((SUBLANES, tc), lambda i, j: (jnp.minimum((i + 1) * rb, n_rb - 1), j)),
                  pl.BlockSpec((ts, tc), lambda i, j: (i, j)),
                  pl.BlockSpec((3, tc), lambda i, j: (0, j))],
        out_specs=pl.BlockSpec((ts, tc), lambda i, j: (i, j)),
        out_shape=jax.ShapeDtypeStruct((m, d), BF16),
        compiler_params=_params(2),
        name="gated_conv",
    )(u, u, u, b_gate, conv_w)


def _ffn(x, h, gate, wg, wu, wo, seq):
    m, d = x.shape
    fp = wg.shape[1]
    tm = min(1024, seq)
    tf = min(512, fp)
    hidden, = _matmul(
        [h], [wg, wu], [], [(jax.ShapeDtypeStruct((m, fp), BF16), _tile_spec(tm, tf))],
        _epi_swiglu, m=m, n=fp, kdim=d, tm=tm, tn=tf, tk=d, name="ffn_in")
    tn = min(1024, d)
    tk = _largest_tile(fp, 3072)
    out, = _matmul(
        [hidden], [wo],
        [(x, _tile_spec(tm, tn)), (gate, _seq_vec_spec(tm, tn, seq))],
        [(jax.ShapeDtypeStruct((m, d), F32), _tile_spec(tm, tn))],
        functools.partial(_epi_residual, 0.5), m=m, n=d, kdim=fp, tm=tm, tn=tn, tk=tk,
        name="ffn_out")
    return out


def _rope_tables(seq):
    inv_freq = ROPE_THETA ** (-jnp.arange(0, HEAD_DIM, 2, dtype=F32) / HEAD_DIM)
    ang = jnp.arange(seq, dtype=F32)[:, None] * inv_freq[None, :]
    cos = jnp.cos(ang)
    sin = jnp.sin(ang)
    return jnp.concatenate([cos, cos], axis=-1), jnp.concatenate([-sin, sin], axis=-1)


def _mixer(x, h, gate, w, bsz, seq, lam_init):
    m, d = x.shape
    w_in = w["w_in"]
    tm = min(1024, seq)

    tn = min(1024, d)
    cos, sin = _rope_tables(seq)
    rope_spec = pl.BlockSpec((tm, HEAD_DIM), lambda i, j, k: (i % (seq // tm), 0))
    qk, = _matmul(
        [h], [w_in], [(cos, rope_spec), (sin, rope_spec)],
        [(jax.ShapeDtypeStruct((m, 2 * d), BF16), _tile_spec(tm, tn))],
        functools.partial(_epi_rope, d // tn, LOG2_E / math.sqrt(HEAD_DIM), tn),
        m=m, n=2 * d, kdim=d, tm=tm, tn=tn, tk=d, name="proj_qk")
    v, = _matmul(
        [h], [w_in], [], [(jax.ShapeDtypeStruct((m, d), BF16), _tile_spec(tm, tn))],
        _epi_cast, m=m, n=d, kdim=d, tm=tm, tn=tn, tk=d, b_col_off=(2 * d // tn,), name="proj_v")
    gates, = _matmul(
        [h], [w_in], [], [(jax.ShapeDtypeStruct((m, 2 * d), F32), _tile_spec(tm, tn))],
        _epi_sigmoid, m=m, n=2 * d, kdim=d, tm=tm, tn=tn, tk=d, b_col_off=(6 * d // tn,),
        name="proj_gates")
    tmc = min(512, seq)
    tnc = min(512, d)
    nb = d // tnc
    b_gate, u = _matmul(
        [h], [w_in, w_in, w_in], [],
        [(jax.ShapeDtypeStruct((m, d), F32), _tile_spec(tmc, tnc)),
         (jax.ShapeDtypeStruct((m, d), F32), _tile_spec(tmc, tnc))],
        _epi_conv_operands, m=m, n=d, kdim=d, tm=tmc, tn=tnc, tk=d,
        b_col_off=(3 * nb, 4 * nb, 5 * nb), name="proj_conv")

    o = _attention(qk, v, w["lambda_qk"], w["g_subln"], bsz=bsz, seq=seq, d=d, lam_init=lam_init)
    z = _gated_conv(u, b_gate, w["conv_w"], seq)

    tmm = min(512, seq)
    tnm = min(512, d)
    merged, = _matmul(
        [o, z], [w["w_branch_attn"], w["w_branch_conv"]],
        [(gates, _tile_spec(tmm, tnm)), (gates, _tile_spec(tmm, tnm, d // tnm))],
        [(jax.ShapeDtypeStruct((m, d), BF16), _tile_spec(tmm, tnm))],
        _epi_merge, m=m, n=d, kdim=d, tm=tmm, tn=tnm, tk=d, pair_a=(0, 1), name="branch_merge")

    out, = _matmul(
        [merged], [w["w_out"]],
        [(x, _tile_spec(tm, tn)), (gate, _seq_vec_spec(tm, tn, seq))],
        [(jax.ShapeDtypeStruct((m, d), F32), _tile_spec(tm, tn))],
        functools.partial(_epi_residual, 1.0), m=m, n=d, kdim=d, tm=tm, tn=tn, tk=d,
        name="out_proj")
    return out


def _encode(x3, mod, layers, g_final):
    bsz, seq, d = x3.shape
    x = x3.reshape(bsz * seq, d)
    for l, w in enumerate(layers):
        lam_init = 0.8 - 0.6 * math.exp(-0.3 * l)
        md = mod[l].reshape(bsz, N_SUBLAYERS, 3, 1, d)
        shift = [md[:, s, 0] for s in range(N_SUBLAYERS)]
        scale = [md[:, s, 1] for s in range(N_SUBLAYERS)]
        gate = [md[:, s, 2] for s in range(N_SUBLAYERS)]
        h = _norm_mod(x, w["g_norm"][0], scale[0], shift[0], seq)
        x = _ffn(x, h, gate[0], *w["ffn"][0], seq)
        h = _norm_mod(x, w["g_norm"][1], scale[1], shift[1], seq)
        x = _mixer(x, h, gate[1], w, bsz, seq, lam_init)
        h = _norm_mod(x, w["g_norm"][2], scale[2], shift[2], seq)
        x = _ffn(x, h, gate[2], *w["ffn"][1], seq)
    return _final_norm(x, g_final, seq).reshape(bsz, seq, d)


def kernel(x_prompt, x_sample, c_prompt, c_sample, w_mod, b_mod, g_norm, w_ffn_in, w_ffn_out, w_in,
           conv_w, lambda_qk, g_subln, w_branch_attn, w_branch_conv, w_out, g_final):
    depth = w_mod.shape[0]
    ffn_dim = w_ffn_out.shape[2]
    fp = _round_up(ffn_dim, FFN_PAD) if ffn_dim > FFN_PAD else ffn_dim
    bp, bs = c_prompt.shape[0], c_sample.shape[0]
    rows = _round_up(bp + bs, SUBLANES)
    c_all = jnp.pad(jnp.concatenate([c_prompt, c_sample], axis=0), ((0, rows - bp - bs), (0, 0)))

    layers, mod_p, mod_s = [], [], []
    for l in range(depth):
        mod = _modulation(c_all, w_mod[l], b_mod[l])
        mod_p.append(mod[:bp])
        mod_s.append(mod[bp:bp + bs])
        ffn = []
        for j in range(2):
            wg = _ffn_in_half(w_ffn_in, l, j, 0, ffn_dim, fp)
            wu = _ffn_in_half(w_ffn_in, l, j, 1, ffn_dim, fp)
            wo = _ffn_out_padded(w_ffn_out, l, j, fp)
            ffn.append((wg, wu, wo))
        layers.append(dict(
            ffn=ffn, g_norm=g_norm[l], w_in=w_in[l].astype(BF16), conv_w=conv_w[l],
            lambda_qk=lambda_qk[l], g_subln=g_subln[l],
            w_branch_attn=w_branch_attn[l].astype(BF16),
            w_branch_conv=w_branch_conv[l].astype(BF16), w_out=w_out[l].astype(BF16)))

    y_prompt = _encode(x_prompt, mod_p, layers, g_final)
    y_sample = _encode(x_sample, mod_s, layers, g_final)
    return (y_prompt, y_sample)
```

```python
import functools
import math

import jax
import jax.numpy as jnp
from jax import lax
from jax.experimental import pallas as pl
from jax.experimental.pallas import tpu as pltpu

HEAD_DIM = 128
NORM_EPS = 1e-6
ROPE_THETA = 10000.0
N_SUBLAYERS = 3
LANES = 128
SUBLANES = 8
BF16_SUBLANES = 16
VMEM_LIMIT_BYTES = 60 * 1024 * 1024
FFN_PAD = 1024
LOG2_E = 1.4426950408889634
V_DOUBLE_BUFFER_MAX_BYTES = 4 * 1024 * 1024
MIXER_WEIGHTS = ("w_in", "w_branch_attn", "w_branch_conv", "w_out")

F32 = jnp.float32
BF16 = jnp.bfloat16


def _params(n_grid_dims, flags=None):
    return pltpu.CompilerParams(
        dimension_semantics=("arbitrary",) * n_grid_dims,
        vmem_limit_bytes=VMEM_LIMIT_BYTES,
        flags=flags,
    )


def _round_up(x, m):
    return (x + m - 1) // m * m


def _largest_tile(total, cap):
    best = LANES
    for t in range(LANES, min(total, cap) + 1, LANES):
        if total % t == 0:
            best = t
    return best


def _cast_pad_cols_kernel(x_ref, o_ref):
    n = x_ref.shape[1]
    o_ref[:, :n] = x_ref[...].astype(o_ref.dtype)
    if o_ref.shape[1] > n:
        o_ref[:, n:] = jnp.zeros((o_ref.shape[0], o_ref.shape[1] - n), o_ref.dtype)


def _cast_pad_rows_kernel(x_ref, o_ref):
    n = x_ref.shape[0]
    o_ref[:n, :] = x_ref[...].astype(o_ref.dtype)
    if o_ref.shape[0] > n:
        o_ref[n:, :] = jnp.zeros((o_ref.shape[0] - n, o_ref.shape[1]), o_ref.dtype)


def _ffn_in_half(w, l, j, half, f, fp):
    d = w.shape[2]
    tr = min(256, d)
    return pl.pallas_call(
        _cast_pad_cols_kernel,
        grid=(d // tr,),
        in_specs=[pl.BlockSpec((None, None, tr, f), lambda i: (l, j, i, half))],
        out_specs=pl.BlockSpec((tr, fp), lambda i: (i, 0)),
        out_shape=jax.ShapeDtypeStruct((d, fp), BF16),
        compiler_params=_params(1),
        name="cast_ffn_in",
    )(w)


def _ffn_out_padded(w, l, j, fp):
    f, d = w.shape[2:]
    tc = min(256, d)
    return pl.pallas_call(
        _cast_pad_rows_kernel,
        grid=(d // tc,),
        in_specs=[pl.BlockSpec((None, None, f, tc), lambda c: (l, j, 0, c))],
        out_specs=pl.BlockSpec((fp, tc), lambda c: (0, c)),
        out_shape=jax.ShapeDtypeStruct((fp, d), BF16),
        compiler_params=_params(1),
        name="cast_ffn_out",
    )(w)


def _mod_kernel(c_ref, w_ref, b_ref, o_ref):
    o_ref[...] = jnp.dot(c_ref[...].astype(BF16), w_ref[...].astype(BF16),
                         preferred_element_type=F32) + b_ref[...]


def _modulation(c_all, w_mod, b_mod):
    rows, d = c_all.shape
    n = w_mod.shape[1]
    tn = min(512, n)
    return pl.pallas_call(
        _mod_kernel,
        grid=(n // tn,),
        in_specs=[pl.BlockSpec((rows, d), lambda j: (0, 0)),
                  pl.BlockSpec((d, tn), lambda j: (0, j)),
                  pl.BlockSpec((1, tn), lambda j: (0, j))],
        out_specs=pl.BlockSpec((rows, tn), lambda j: (0, j)),
        out_shape=jax.ShapeDtypeStruct((rows, n), F32),
        compiler_params=_params(1),
        name="modulation",
    )(c_all, w_mod, b_mod.reshape(1, n))


def _norm_mod_kernel(x_ref, g_ref, scale_ref, shift_ref, o_ref):
    x = x_ref[...]
    ms = jnp.mean(x * x, axis=-1, keepdims=True)
    xn = (x * lax.rsqrt(ms + NORM_EPS)) * g_ref[...]
    o_ref[...] = (xn * (1.0 + scale_ref[...]) + shift_ref[...]).astype(o_ref.dtype)


def _norm_kernel(x_ref, g_ref, o_ref):
    x = x_ref[...]
    ms = jnp.mean(x * x, axis=-1, keepdims=True)
    o_ref[...] = ((x * lax.rsqrt(ms + NORM_EPS)) * g_ref[...]).astype(o_ref.dtype)


def _norm_mod(x, g, scale, shift, seq):
    m, d = x.shape
    tm = min(512, seq)
    vec = pl.BlockSpec((None, 1, d), lambda i: (i * tm // seq, 0, 0))
    return pl.pallas_call(
        _norm_mod_kernel,
        grid=(m // tm,),
        in_specs=[pl.BlockSpec((tm, d), lambda i: (i, 0)),
                  pl.BlockSpec((1, d), lambda i: (0, 0)), vec, vec],
        out_specs=pl.BlockSpec((tm, d), lambda i: (i, 0)),
        out_shape=jax.ShapeDtypeStruct((m, d), BF16),
        compiler_params=_params(1),
        name="norm_mod",
    )(x, g.reshape(1, d), scale, shift)


def _final_norm(x, g, seq):
    m, d = x.shape
    tm = min(512, seq)
    return pl.pallas_call(
        _norm_kernel,
        grid=(m // tm,),
        in_specs=[pl.BlockSpec((tm, d), lambda i: (i, 0)),
                  pl.BlockSpec((1, d), lambda i: (0, 0))],
        out_specs=pl.BlockSpec((tm, d), lambda i: (i, 0)),
        out_shape=jax.ShapeDtypeStruct((m, d), F32),
        compiler_params=_params(1),
        name="final_norm",
    )(x, g.reshape(1, d))


def _mm_body(*refs, n_a, n_b, n_e, n_s, n_o, nk, pair_a, epilogue):
    a = refs[:n_a]
    b = refs[n_a:n_a + n_b]
    e = refs[n_a + n_b:n_a + n_b + n_e]
    side_in = refs[n_a + n_b + n_e:n_a + n_b + n_e + n_s]
    n_in = n_a + n_b + n_e + n_s
    o = refs[n_in:n_in + n_o]
    side_out = refs[n_in + n_o:n_in + n_o + n_s]
    acc = refs[n_in + n_o + n_s:]
    ids = (pl.program_id(0), pl.program_id(1))

    for s_in, s_out in zip(side_in, side_out):
        s_out[...] = s_in[...].astype(s_out.dtype)

    def products():
        return [jnp.dot(a[pair_a[t]][...], b[t][...], preferred_element_type=F32)
                for t in range(n_b)]

    if nk == 1:
        epilogue(ids, products(), e, o)
        return
    k = pl.program_id(2)

    @pl.when(k == 0)
    def _():
        for t, r in enumerate(products()):
            acc[t][...] = r

    @pl.when(jnp.logical_and(k > 0, k < nk - 1))
    def _():
        for t, r in enumerate(products()):
            acc[t][...] += r

    @pl.when(k == nk - 1)
    def _():
        epilogue(ids, [acc[t][...] + r for t, r in enumerate(products())], e, o)


def _side_cast_rows(rows, n_steps):
    for tr in range(BF16_SUBLANES, rows + 1, BF16_SUBLANES):
        if rows % tr == 0 and rows // tr <= n_steps:
            return tr
    return None


def _matmul(a_arrs, b_arrs, extras, outs, epilogue, *, m, n, kdim, tm, tn, tk,
            pair_a=None, b_col_off=None, side_casts=(), name):
    n_b = len(b_arrs)
    pair_a = tuple(pair_a) if pair_a is not None else (0,) * n_b
    b_col_off = tuple(b_col_off) if b_col_off is not None else (0,) * n_b
    assert m % tm == 0 and n % tn == 0 and kdim % tk == 0
    nj, nk = n // tn, kdim // tk
    n_steps = (m // tm) * nj * nk
    in_specs = [pl.BlockSpec((tm, tk), lambda i, j, k: (i, k)) for _ in a_arrs]
    for off in b_col_off:
        in_specs.append(pl.BlockSpec((tk, tn), functools.partial(
            lambda i, j, k, off: (k, j + off), off=off)))
    in_specs += [s for _, s in extras]
    out_specs = [s for _, s in outs]
    out_shape = [sh for sh, _ in outs]
    side_specs = []
    for w in side_casts:
        rows, cols = w.shape
        tr = _side_cast_rows(rows, n_steps)
        assert tr is not None
        spec = pl.BlockSpec((tr, cols), functools.partial(
            lambda i, j, k, last: (jnp.minimum((i * nj + j) * nk + k, last), 0),
            last=rows // tr - 1))
        side_specs.append(spec)
        out_shape.append(jax.ShapeDtypeStruct((rows, cols), BF16))
    scratch = [pltpu.VMEM((tm, tn), F32) for _ in range(n_b)] if nk > 1 else []
    body = functools.partial(_mm_body, n_a=len(a_arrs), n_b=n_b, n_e=len(extras),
                             n_s=len(side_casts), n_o=len(outs), nk=nk, pair_a=pair_a,
                             epilogue=epilogue)
    res = pl.pallas_call(
        body,
        grid=(m // tm, nj, nk),
        in_specs=in_specs + side_specs,
        out_specs=out_specs + side_specs,
        out_shape=out_shape,
        scratch_shapes=scratch,
        compiler_params=_params(3),
        name=name,
    )(*a_arrs, *b_arrs, *[x for x, _ in extras], *side_casts)
    return res


def _tile_spec(tm, tn, col_off=0):
    return pl.BlockSpec((tm, tn), lambda i, j, k: (i, j + col_off))


def _seq_vec_spec(tm, tn, seq):
    return pl.BlockSpec((None, 1, tn), lambda i, j, k: (i * tm // seq, 0, j))


def _epi_swiglu(ids, r, e, o):
    g, u = r
    o[0][...] = ((g * jax.nn.sigmoid(g)) * u).astype(o[0].dtype)


def _epi_residual(coef, ids, r, e, o):
    x_ref, gate_ref = e
    gate = gate_ref[...] if coef == 1.0 else coef * gate_ref[...]
    o[0][...] = x_ref[...] + gate * r[0]


def _epi_rope(n_q_tiles, q_scale, ids, r, e, o):
    cos = e[0][...]
    sin = e[1][...]
    s = jnp.where(ids[1] < n_q_tiles, q_scale, 1.0).astype(F32)
    acc = r[0]
    for g in range(acc.shape[1] // HEAD_DIM):
        blk = acc[:, g * HEAD_DIM:(g + 1) * HEAD_DIM]
        rot = pltpu.roll(blk, HEAD_DIM // 2, 1)
        o[0][:, g * HEAD_DIM:(g + 1) * HEAD_DIM] = ((blk * cos + rot * sin) * s).astype(o[0].dtype)


def _epi_cast(ids, r, e, o):
    o[0][...] = r[0].astype(o[0].dtype)


def _epi_conv_operands(ids, r, e, o):
    o[0][...] = r[0]
    o[1][...] = r[1] * r[2]


def _epi_sigmoid(ids, r, e, o):
    o[0][...] = jax.nn.sigmoid(r[0])


def _epi_merge(ids, r, e, o):
    o[0][...] = (e[0][...] * r[0] + e[1][...] * r[1]).astype(o[0].dtype)


def _attn_kernel(lq_ref, g_ref, q_ref, k_ref, v_ref, o_ref, vt_ref, acc_ref, sa_ref, sb_ref, *,
                 tk, lam_init):
    tq = q_ref.shape[0]
    seq = k_ref.shape[0]
    nk = seq // tk

    @pl.when(pl.program_id(2) == 0)
    def _():
        def tr(j, carry):
            start = pl.multiple_of(j * tk, tk)
            vt_ref[j] = v_ref[pl.ds(start, tk), :].astype(F32).T.astype(BF16)
            return carry
        lax.fori_loop(0, nk, tr, 0)

    q = (q_ref[:, :HEAD_DIM], q_ref[:, HEAD_DIM:])
    acc_ref[...] = jnp.zeros_like(acc_ref)

    def scores(j, s_ref):
        start = pl.multiple_of(j * tk, tk)
        kblk = k_ref[pl.ds(start, tk), :]
        cmax = []
        for c in range(2):
            st = lax.dot_general(kblk[:, c * HEAD_DIM:(c + 1) * HEAD_DIM], q[c],
                                 (((1,), (1,)), ((), ())), preferred_element_type=F32)
            s_ref[c] = st
            cmax.append(jnp.max(st, axis=0, keepdims=True))
        return cmax

    def consume(j, s_ref, cmax, stats):
        vt = vt_ref[j]
        out = []
        for c in range(2):
            m, l = stats[c]
            m_new = jnp.maximum(m, cmax[c])
            alpha = jnp.exp2(m - m_new)
            p = jnp.exp2(s_ref[c] - m_new)
            l_new = alpha * l + jnp.sum(p, axis=0, keepdims=True)
            acc_ref[c] = acc_ref[c] * alpha + jnp.dot(vt, p.astype(BF16), preferred_element_type=F32)
            out.append((m_new, l_new))
        return out

    def body(jj, carry):
        stats, cmax_a = carry
        ja = 2 * jj
        cmax_b = scores(ja + 1, sb_ref)
        stats = consume(ja, sa_ref, cmax_a, stats)
        cmax_a = scores(ja + 2, sa_ref)
        stats = consume(ja + 1, sb_ref, cmax_b, stats)
        return stats, cmax_a

    neg = jnp.full((1, tq), -jnp.inf, F32)
    zero = jnp.zeros((1, tq), F32)
    cmax_a = scores(0, sa_ref)
    stats, cmax_a = lax.fori_loop(0, nk // 2 - 1, body, ([(neg, zero), (neg, zero)], cmax_a),
                                  unroll=2)
    cmax_b = scores(nk - 1, sb_ref)
    stats = consume(nk - 2, sa_ref, cmax_a, stats)
    stats = consume(nk - 1, sb_ref, cmax_b, stats)
    l0, l1 = stats[0][1], stats[1][1]

    lq = lq_ref[...]
    lam = (jnp.exp(jnp.sum(lq[0:1] * lq[1:2], axis=-1, keepdims=True))
           - jnp.exp(jnp.sum(lq[2:3] * lq[3:4], axis=-1, keepdims=True)) + lam_init)
    ot = acc_ref[0] / l0 - lam * (acc_ref[1] / l1)
    ms = jnp.mean(ot * ot, axis=0, keepdims=True)
    o = (ot * lax.rsqrt(ms + NORM_EPS)).T * g_ref[...]
    o_ref[...] = (o * (1.0 - lam_init)).astype(o_ref.dtype)


def _attention(qk, v, lambda_qk, g_subln, *, bsz, seq, d, lam_init):
    heads = d // (2 * HEAD_DIM)
    hw = 2 * HEAD_DIM
    tq = min(512, seq)
    tk = min(1024, seq // 2)
    assert seq % (2 * tk) == 0
    nq = seq // tq
    kern = functools.partial(_attn_kernel, tk=tk, lam_init=lam_init)
    v_block_bytes = seq * hw * 2
    v_mode = pl.Buffered(1) if v_block_bytes > V_DOUBLE_BUFFER_MAX_BYTES else None
    return pl.pallas_call(
        kern,
        grid=(bsz, heads, nq),
        in_specs=[pl.BlockSpec((4, HEAD_DIM), lambda b, h, i: (0, 0)),
                  pl.BlockSpec((1, hw), lambda b, h, i: (0, 0)),
                  pl.BlockSpec((tq, hw), lambda b, h, i: (b * nq + i, h)),
                  pl.BlockSpec((seq, hw), lambda b, h, i: (b, heads + h)),
                  pl.BlockSpec((seq, hw), lambda b, h, i: (b, h), pipeline_mode=v_mode)],
        out_specs=pl.BlockSpec((tq, hw), lambda b, h, i: (b * nq + i, h)),
        out_shape=jax.ShapeDtypeStruct((bsz * seq, d), BF16),
        scratch_shapes=[pltpu.VMEM((seq // tk, hw, tk), BF16), pltpu.VMEM((2, hw, tq), F32),
                        pltpu.VMEM((2, tk, tq), F32), pltpu.VMEM((2, tk, tq), F32)],
        compiler_params=_params(3),
        name="diff_attention",
    )(lambda_qk, g_subln.reshape(1, hw), qk, qk, v)


def _conv_kernel(u_ref, prev_ref, next_ref, b_ref, w_ref, o_ref, *, tiles_per_seq):
    i = pl.program_id(0)
    ts = u_ref.shape[0]
    u = u_ref[...]
    pos = i % tiles_per_seq
    prev_row = jnp.where(pos == 0, 0.0, prev_ref[SUBLANES - 1:SUBLANES, :])
    next_row = jnp.where(pos == tiles_per_seq - 1, 0.0, next_ref[0:1, :])
    row = lax.broadcasted_iota(jnp.int32, u.shape, 0)
    u_prev = jnp.where(row == 0, prev_row, pltpu.roll(u, 1, 0))
    u_next = jnp.where(row == ts - 1, next_row, pltpu.roll(u, ts - 1, 0))
    w = w_ref[...]
    y = u_prev * w[0:1] + u * w[1:2] + u_next * w[2:3]
    o_ref[...] = (b_ref[...] * y).astype(o_ref.dtype)


def _gated_conv(u, b_gate, conv_w, seq):
    m, d = u.shape
    ts = min(512, seq)
    tc = min(1024, d)
    rb = ts // SUBLANES
    n_rb = m // SUBLANES
    kern = functools.partial(_conv_kernel, tiles_per_seq=seq // ts)
    return pl.pallas_call(
        kern,
        grid=(m // ts, d // tc),
        in_specs=[pl.BlockSpec((ts, tc), lambda i, j: (i, j)),
                  pl.BlockSpec((SUBLANES, tc), lambda i, j: (jnp.maximum(i * rb - 1, 0), j)),
                  pl.BlockSpec((SUBLANES, tc), lambda i, j: (jnp.minimum((i + 1) * rb, n_rb - 1), j)),
                  pl.BlockSpec((ts, tc), lambda i, j: (i, j)),
                  pl.BlockSpec((3, tc), lambda i, j: (0, j))],
        out_specs=pl.BlockSpec((ts, tc), lambda i, j: (i, j)),
        out_shape=jax.ShapeDtypeStruct((m, d), BF16),
        compiler_params=_params(2),
        name="gated_conv",
    )(u, u, u, b_gate, conv_w)


def _ffn(x, h, gate, wg, wu, wo, seq, side_casts=()):
    m, d = x.shape
    fp = wg.shape[1]
    tm = min(1024, seq)
    tf = min(256 if side_casts else 512, fp)
    n_steps = (m // tm) * (fp // tf)
    rides = [_side_cast_rows(w.shape[0], n_steps) is not None for w in side_casts]
    hidden, *riding = _matmul(
        [h], [wg, wu], [], [(jax.ShapeDtypeStruct((m, fp), BF16), _tile_spec(tm, tf))],
        _epi_swiglu, m=m, n=fp, kdim=d, tm=tm, tn=tf, tk=d,
        side_casts=[w for w, r in zip(side_casts, rides) if r], name="ffn_in")
    riding = iter(riding)
    cast = [next(riding) if r else w.astype(BF16) for w, r in zip(side_casts, rides)]
    tn = min(1024, d)
    tk = _largest_tile(fp, 3072)
    out, = _matmul(
        [hidden], [wo],
        [(x, _tile_spec(tm, tn)), (gate, _seq_vec_spec(tm, tn, seq))],
        [(jax.ShapeDtypeStruct((m, d), F32), _tile_spec(tm, tn))],
        functools.partial(_epi_residual, 0.5), m=m, n=d, kdim=fp, tm=tm, tn=tn, tk=tk,
        name="ffn_out")
    return out, cast


def _rope_tables(seq):
    inv_freq = ROPE_THETA ** (-jnp.arange(0, HEAD_DIM, 2, dtype=F32) / HEAD_DIM)
    ang = jnp.arange(seq, dtype=F32)[:, None] * inv_freq[None, :]
    cos = jnp.cos(ang)
    sin = jnp.sin(ang)
    return jnp.concatenate([cos, cos], axis=-1), jnp.concatenate([-sin, sin], axis=-1)


def _mixer(x, h, gate, w, bsz, seq, lam_init):
    m, d = x.shape
    w_in = w["w_in"]
    tm = min(1024, seq)

    tn = min(1024, d)
    cos, sin = _rope_tables(seq)
    rope_spec = pl.BlockSpec((tm, HEAD_DIM), lambda i, j, k: (i % (seq // tm), 0))
    qk, = _matmul(
        [h], [w_in], [(cos, rope_spec), (sin, rope_spec)],
        [(jax.ShapeDtypeStruct((m, 2 * d), BF16), _tile_spec(tm, tn))],
        functools.partial(_epi_rope, d // tn, LOG2_E / math.sqrt(HEAD_DIM)),
        m=m, n=2 * d, kdim=d, tm=tm, tn=tn, tk=d, name="proj_qk")
    v, = _matmul(
        [h], [w_in], [], [(jax.ShapeDtypeStruct((m, d), BF16), _tile_spec(tm, tn))],
        _epi_cast, m=m, n=d, kdim=d, tm=tm, tn=tn, tk=d, b_col_off=(2 * d // tn,), name="proj_v")
    gates, = _matmul(
        [h], [w_in], [], [(jax.ShapeDtypeStruct((m, 2 * d), F32), _tile_spec(tm, tn))],
        _epi_sigmoid, m=m, n=2 * d, kdim=d, tm=tm, tn=tn, tk=d, b_col_off=(6 * d // tn,),
        name="proj_gates")
    tmc = min(512, seq)
    tnc = min(512, d)
    nb = d // tnc
    b_gate, u = _matmul(
        [h], [w_in, w_in, w_in], [],
        [(jax.ShapeDtypeStruct((m, d), F32), _tile_spec(tmc, tnc)),
         (jax.ShapeDtypeStruct((m, d), F32), _tile_spec(tmc, tnc))],
        _epi_conv_operands, m=m, n=d, kdim=d, tm=tmc, tn=tnc, tk=d,
        b_col_off=(3 * nb, 4 * nb, 5 * nb), name="proj_conv")

    o = _attention(qk, v, w["lambda_qk"], w["g_subln"], bsz=bsz, seq=seq, d=d, lam_init=lam_init)
    z = _gated_conv(u, b_gate, w["conv_w"], seq)

    tmm = min(1024, seq)
    tnm = min(256, d)
    merged, = _matmul(
        [o, z], [w["w_branch_attn"], w["w_branch_conv"]],
        [(gates, _tile_spec(tmm, tnm)), (gates, _tile_spec(tmm, tnm, d // tnm))],
        [(jax.ShapeDtypeStruct((m, d), BF16), _tile_spec(tmm, tnm))],
        _epi_merge, m=m, n=d, kdim=d, tm=tmm, tn=tnm, tk=d, pair_a=(0, 1), name="branch_merge")

    out, = _matmul(
        [merged], [w["w_out"]],
        [(x, _tile_spec(tm, tn)), (gate, _seq_vec_spec(tm, tn, seq))],
        [(jax.ShapeDtypeStruct((m, d), F32), _tile_spec(tm, tn))],
        functools.partial(_epi_residual, 1.0), m=m, n=d, kdim=d, tm=tm, tn=tn, tk=d,
        name="out_proj")
    return out


def _encode(x3, mod, layers, g_final):
    bsz, seq, d = x3.shape
    x = x3.reshape(bsz * seq, d)
    for l, w in enumerate(layers):
        lam_init = 0.8 - 0.6 * math.exp(-0.3 * l)
        md = mod[l].reshape(bsz, N_SUBLAYERS, 3, 1, d)
        shift = [md[:, s, 0] for s in range(N_SUBLAYERS)]
        scale = [md[:, s, 1] for s in range(N_SUBLAYERS)]
        gate = [md[:, s, 2] for s in range(N_SUBLAYERS)]
        h = _norm_mod(x, w["g_norm"][0], scale[0], shift[0], seq)
        pending = [name for name in MIXER_WEIGHTS if w[name].dtype != BF16]
        x, cast = _ffn(x, h, gate[0], *w["ffn"][0], seq, side_casts=[w[name] for name in pending])
        w.update(zip(pending, cast))
        h = _norm_mod(x, w["g_norm"][1], scale[1], shift[1], seq)
        x = _mixer(x, h, gate[1], w, bsz, seq, lam_init)
        h = _norm_mod(x, w["g_norm"][2], scale[2], shift[2], seq)
        x, _ = _ffn(x, h, gate[2], *w["ffn"][1], seq)
    return _final_norm(x, g_final, seq).reshape(bsz, seq, d)


def kernel(x_prompt, x_sample, c_prompt, c_sample, w_mod, b_mod, g_norm, w_ffn_in, w_ffn_out, w_in,
           conv_w, lambda_qk, g_subln, w_branch_attn, w_branch_conv, w_out, g_final):
    depth = w_mod.shape[0]
    ffn_dim = w_ffn_out.shape[2]
    fp = _round_up(ffn_dim, FFN_PAD) if ffn_dim > FFN_PAD else ffn_dim
    bp, bs = c_prompt.shape[0], c_sample.shape[0]
    rows = _round_up(bp + bs, SUBLANES)
    c_all = jnp.pad(jnp.concatenate([c_prompt, c_sample], axis=0), ((0, rows - bp - bs), (0, 0)))

    layers, mod_p, mod_s = [], [], []
    for l in range(depth):
        mod = _modulation(c_all, w_mod[l], b_mod[l])
        mod_p.append(mod[:bp])
        mod_s.append(mod[bp:bp + bs])
        ffn = []
        for j in range(2):
            wg = _ffn_in_half(w_ffn_in, l, j, 0, ffn_dim, fp)
            wu = _ffn_in_half(w_ffn_in, l, j, 1, ffn_dim, fp)
            wo = _ffn_out_padded(w_ffn_out, l, j, fp)
            ffn.append((wg, wu, wo))
        layers.append(dict(
            ffn=ffn, g_norm=g_norm[l], w_in=w_in[l], conv_w=conv_w[l],
            lambda_qk=lambda_qk[l], g_subln=g_subln[l], w_branch_attn=w_branch_attn[l],
            w_branch_conv=w_branch_conv[l], w_out=w_out[l]))

    y_prompt = _encode(x_prompt, mod_p, layers, g_final)
    y_sample = _encode(x_sample, mod_s, layers, g_final)
    return (y_prompt, y_sample)
```

```python
import functools
import math

import jax
import jax.numpy as jnp
from jax import lax
from jax.experimental import pallas as pl
from jax.experimental.pallas import tpu as pltpu

HEAD_DIM = 128
NORM_EPS = 1e-6
ROPE_THETA = 10000.0
N_SUBLAYERS = 3
LANES = 128
SUBLANES = 8
BF16_SUBLANES = 16
VMEM_LIMIT_BYTES = 60 * 1024 * 1024
FFN_PAD = 1024
LOG2_E = 1.4426950408889634
ATTN_SHORT_SEQ = 4096
ATTN_KEY_CHUNK_SHORT = 1024
ATTN_KEY_CHUNK_LONG = 512
V_DOUBLE_BUFFER_MAX_BYTES = 4 * 1024 * 1024
MIXER_WEIGHTS = ("w_in", "w_branch_attn", "w_branch_conv", "w_out")

F32 = jnp.float32
BF16 = jnp.bfloat16


def _params(n_grid_dims, flags=None):
    return pltpu.CompilerParams(
        dimension_semantics=("arbitrary",) * n_grid_dims,
        vmem_limit_bytes=VMEM_LIMIT_BYTES,
        flags=flags,
    )


def _round_up(x, m):
    return (x + m - 1) // m * m


def _largest_tile(total, cap):
    best = LANES
    for t in range(LANES, min(total, cap) + 1, LANES):
        if total % t == 0:
            best = t
    return best


def _cast_pad_cols_kernel(x_ref, o_ref):
    n = x_ref.shape[1]
    o_ref[:, :n] = x_ref[...].astype(o_ref.dtype)
    if o_ref.shape[1] > n:
        o_ref[:, n:] = jnp.zeros((o_ref.shape[0], o_ref.shape[1] - n), o_ref.dtype)


def _cast_pad_rows_kernel(x_ref, o_ref):
    n = x_ref.shape[0]
    o_ref[:n, :] = x_ref[...].astype(o_ref.dtype)
    if o_ref.shape[0] > n:
        o_ref[n:, :] = jnp.zeros((o_ref.shape[0] - n, o_ref.shape[1]), o_ref.dtype)


def _ffn_in_half(w, l, j, half, f, fp):
    d = w.shape[2]
    tr = min(256, d)
    return pl.pallas_call(
        _cast_pad_cols_kernel,
        grid=(d // tr,),
        in_specs=[pl.BlockSpec((None, None, tr, f), lambda i: (l, j, i, half))],
        out_specs=pl.BlockSpec((tr, fp), lambda i: (i, 0)),
        out_shape=jax.ShapeDtypeStruct((d, fp), BF16),
        compiler_params=_params(1),
        name="cast_ffn_in",
    )(w)


def _ffn_out_padded(w, l, j, fp):
    f, d = w.shape[2:]
    tc = min(256, d)
    return pl.pallas_call(
        _cast_pad_rows_kernel,
        grid=(d // tc,),
        in_specs=[pl.BlockSpec((None, None, f, tc), lambda c: (l, j, 0, c))],
        out_specs=pl.BlockSpec((fp, tc), lambda c: (0, c)),
        out_shape=jax.ShapeDtypeStruct((fp, d), BF16),
        compiler_params=_params(1),
        name="cast_ffn_out",
    )(w)


def _mod_kernel(c_ref, w_ref, b_ref, o_ref):
    o_ref[...] = jnp.dot(c_ref[...].astype(BF16), w_ref[...].astype(BF16),
                         preferred_element_type=F32) + b_ref[...]


def _modulation(c_all, w_mod, b_mod):
    rows, d = c_all.shape
    n = w_mod.shape[1]
    tn = min(512, n)
    return pl.pallas_call(
        _mod_kernel,
        grid=(n // tn,),
        in_specs=[pl.BlockSpec((rows, d), lambda j: (0, 0)),
                  pl.BlockSpec((d, tn), lambda j: (0, j)),
                  pl.BlockSpec((1, tn), lambda j: (0, j))],
        out_specs=pl.BlockSpec((rows, tn), lambda j: (0, j)),
        out_shape=jax.ShapeDtypeStruct((rows, n), F32),
        compiler_params=_params(1),
        name="modulation",
    )(c_all, w_mod, b_mod.reshape(1, n))


def _norm_mod_kernel(x_ref, g_ref, scale_ref, shift_ref, o_ref):
    x = x_ref[...]
    ms = jnp.mean(x * x, axis=-1, keepdims=True)
    xn = (x * lax.rsqrt(ms + NORM_EPS)) * g_ref[...]
    o_ref[...] = (xn * (1.0 + scale_ref[...]) + shift_ref[...]).astype(o_ref.dtype)


def _norm_kernel(x_ref, g_ref, o_ref):
    x = x_ref[...]
    ms = jnp.mean(x * x, axis=-1, keepdims=True)
    o_ref[...] = ((x * lax.rsqrt(ms + NORM_EPS)) * g_ref[...]).astype(o_ref.dtype)


def _norm_mod(x, g, scale, shift, seq):
    m, d = x.shape
    tm = min(512, seq)
    vec = pl.BlockSpec((None, 1, d), lambda i: (i * tm // seq, 0, 0))
    return pl.pallas_call(
        _norm_mod_kernel,
        grid=(m // tm,),
        in_specs=[pl.BlockSpec((tm, d), lambda i: (i, 0)),
                  pl.BlockSpec((1, d), lambda i: (0, 0)), vec, vec],
        out_specs=pl.BlockSpec((tm, d), lambda i: (i, 0)),
        out_shape=jax.ShapeDtypeStruct((m, d), BF16),
        compiler_params=_params(1),
        name="norm_mod",
    )(x, g.reshape(1, d), scale, shift)


def _final_norm(x, g, seq):
    m, d = x.shape
    tm = min(512, seq)
    return pl.pallas_call(
        _norm_kernel,
        grid=(m // tm,),
        in_specs=[pl.BlockSpec((tm, d), lambda i: (i, 0)),
                  pl.BlockSpec((1, d), lambda i: (0, 0))],
        out_specs=pl.BlockSpec((tm, d), lambda i: (i, 0)),
        out_shape=jax.ShapeDtypeStruct((m, d), F32),
        compiler_params=_params(1),
        name="final_norm",
    )(x, g.reshape(1, d))


def _mm_body(*refs, n_a, n_b, n_e, n_s, n_o, nk, pair_a, epilogue):
    a = refs[:n_a]
    b = refs[n_a:n_a + n_b]
    e = refs[n_a + n_b:n_a + n_b + n_e]
    side_in = refs[n_a + n_b + n_e:n_a + n_b + n_e + n_s]
    n_in = n_a + n_b + n_e + n_s
    o = refs[n_in:n_in + n_o]
    side_out = refs[n_in + n_o:n_in + n_o + n_s]
    acc = refs[n_in + n_o + n_s:]
    ids = (pl.program_id(0), pl.program_id(1))

    for s_in, s_out in zip(side_in, side_out):
        s_out[...] = s_in[...].astype(s_out.dtype)

    def products():
        return [jnp.dot(a[pair_a[t]][...], b[t][...], preferred_element_type=F32)
                for t in range(n_b)]

    if nk == 1:
        epilogue(ids, products(), e, o)
        return
    k = pl.program_id(2)

    @pl.when(k == 0)
    def _():
        for t, r in enumerate(products()):
            acc[t][...] = r

    @pl.when(jnp.logical_and(k > 0, k < nk - 1))
    def _():
        for t, r in enumerate(products()):
            acc[t][...] += r

    @pl.when(k == nk - 1)
    def _():
        epilogue(ids, [acc[t][...] + r for t, r in enumerate(products())], e, o)


def _side_cast_rows(rows, n_steps):
    for tr in range(BF16_SUBLANES, rows + 1, BF16_SUBLANES):
        if rows % tr == 0 and rows // tr <= n_steps:
            return tr
    return None


def _matmul(a_arrs, b_arrs, extras, outs, epilogue, *, m, n, kdim, tm, tn, tk,
            pair_a=None, b_col_off=None, side_casts=(), name):
    n_b = len(b_arrs)
    pair_a = tuple(pair_a) if pair_a is not None else (0,) * n_b
    b_col_off = tuple(b_col_off) if b_col_off is not None else (0,) * n_b
    assert m % tm == 0 and n % tn == 0 and kdim % tk == 0
    nj, nk = n // tn, kdim // tk
    n_steps = (m // tm) * nj * nk
    in_specs = [pl.BlockSpec((tm, tk), lambda i, j, k: (i, k)) for _ in a_arrs]
    for off in b_col_off:
        in_specs.append(pl.BlockSpec((tk, tn), functools.partial(
            lambda i, j, k, off: (k, j + off), off=off)))
    in_specs += [s for _, s in extras]
    out_specs = [s for _, s in outs]
    out_shape = [sh for sh, _ in outs]
    side_specs = []
    for w in side_casts:
        rows, cols = w.shape
        tr = _side_cast_rows(rows, n_steps)
        assert tr is not None
        spec = pl.BlockSpec((tr, cols), functools.partial(
            lambda i, j, k, last: (jnp.minimum((i * nj + j) * nk + k, last), 0),
            last=rows // tr - 1))
        side_specs.append(spec)
        out_shape.append(jax.ShapeDtypeStruct((rows, cols), BF16))
    scratch = [pltpu.VMEM((tm, tn), F32) for _ in range(n_b)] if nk > 1 else []
    body = functools.partial(_mm_body, n_a=len(a_arrs), n_b=n_b, n_e=len(extras),
                             n_s=len(side_casts), n_o=len(outs), nk=nk, pair_a=pair_a,
                             epilogue=epilogue)
    res = pl.pallas_call(
        body,
        grid=(m // tm, nj, nk),
        in_specs=in_specs + side_specs,
        out_specs=out_specs + side_specs,
        out_shape=out_shape,
        scratch_shapes=scratch,
        compiler_params=_params(3),
        name=name,
    )(*a_arrs, *b_arrs, *[x for x, _ in extras], *side_casts)
    return res


def _tile_spec(tm, tn, col_off=0):
    return pl.BlockSpec((tm, tn), lambda i, j, k: (i, j + col_off))


def _seq_vec_spec(tm, tn, seq):
    return pl.BlockSpec((None, 1, tn), lambda i, j, k: (i * tm // seq, 0, j))


def _epi_swiglu(ids, r, e, o):
    g, u = r
    o[0][...] = ((g * jax.nn.sigmoid(g)) * u).astype(o[0].dtype)


def _epi_residual(coef, ids, r, e, o):
    x_ref, gate_ref = e
    gate = gate_ref[...] if coef == 1.0 else coef * gate_ref[...]
    o[0][...] = x_ref[...] + gate * r[0]


def _epi_rope(n_q_tiles, q_scale, ids, r, e, o):
    cos = e[0][...]
    sin = e[1][...]
    s = jnp.where(ids[1] < n_q_tiles, q_scale, 1.0).astype(F32)
    acc = r[0]
    for g in range(acc.shape[1] // HEAD_DIM):
        blk = acc[:, g * HEAD_DIM:(g + 1) * HEAD_DIM]
        rot = pltpu.roll(blk, HEAD_DIM // 2, 1)
        o[0][:, g * HEAD_DIM:(g + 1) * HEAD_DIM] = ((blk * cos + rot * sin) * s).astype(o[0].dtype)


def _epi_cast(ids, r, e, o):
    o[0][...] = r[0].astype(o[0].dtype)


def _epi_conv_operands(ids, r, e, o):
    o[0][...] = r[0]
    o[1][...] = r[1] * r[2]


def _epi_sigmoid(ids, r, e, o):
    o[0][...] = jax.nn.sigmoid(r[0])


def _epi_merge(ids, r, e, o):
    o[0][...] = (e[0][...] * r[0] + e[1][...] * r[1]).astype(o[0].dtype)


def _attn_kernel(lq_ref, g_ref, q_ref, k_ref, v_ref, o_ref, vt_ref, acc_ref, sa_ref, sb_ref, *,
                 tk, lam_init):
    n_sub, _, _, tq = acc_ref.shape
    seq = k_ref.shape[0]
    nk = seq // tk

    @pl.when(pl.program_id(2) == 0)
    def _():
        def tr(j, carry):
            start = pl.multiple_of(j * tk, tk)
            vt_ref[j] = v_ref[pl.ds(start, tk), :].astype(F32).T.astype(BF16)
            return carry
        lax.fori_loop(0, nk, tr, 0)

    lq = lq_ref[...]
    lam = (jnp.exp(jnp.sum(lq[0:1] * lq[1:2], axis=-1, keepdims=True))
           - jnp.exp(jnp.sum(lq[2:3] * lq[3:4], axis=-1, keepdims=True)) + lam_init)

    def scores(q, j, s_ref):
        start = pl.multiple_of(j * tk, tk)
        kblk = k_ref[pl.ds(start, tk), :]
        cmax = []
        for c in range(2):
            st = lax.dot_general(kblk[:, c * HEAD_DIM:(c + 1) * HEAD_DIM], q[c],
                                 (((1,), (1,)), ((), ())), preferred_element_type=F32)
            s_ref[c] = st
            cmax.append(jnp.max(st, axis=0, keepdims=True))
        return cmax

    def consume(acc, j, s_ref, cmax, stats):
        vt = vt_ref[j]
        out = []
        for c in range(2):
            m, l = stats[c]
            m_new = jnp.maximum(m, cmax[c])
            alpha = jnp.exp2(m - m_new)
            p = jnp.exp2(s_ref[c] - m_new)
            l_new = alpha * l + jnp.sum(p, axis=0, keepdims=True)
            acc[c] = acc[c] * alpha + jnp.dot(vt, p.astype(BF16), preferred_element_type=F32)
            out.append((m_new, l_new))
        return out

    for t in range(n_sub):
        rows = slice(t * tq, (t + 1) * tq)
        q = (q_ref[rows, :HEAD_DIM], q_ref[rows, HEAD_DIM:])
        acc = acc_ref.at[t]
        acc[...] = jnp.zeros(acc.shape, F32)

        def body(jj, carry, q=q, acc=acc):
            stats, cmax_a = carry
            ja = 2 * jj
            cmax_b = scores(q, ja + 1, sb_ref)
            stats = consume(acc, ja, sa_ref, cmax_a, stats)
            cmax_a = scores(q, ja + 2, sa_ref)
            stats = consume(acc, ja + 1, sb_ref, cmax_b, stats)
            return stats, cmax_a

        neg = jnp.full((1, tq), -jnp.inf, F32)
        zero = jnp.zeros((1, tq), F32)
        cmax_a = scores(q, 0, sa_ref)
        stats, cmax_a = lax.fori_loop(0, nk // 2 - 1, body, ([(neg, zero), (neg, zero)], cmax_a),
                                      unroll=4)
        cmax_b = scores(q, nk - 1, sb_ref)
        stats = consume(acc, nk - 2, sa_ref, cmax_a, stats)
        stats = consume(acc, nk - 1, sb_ref, cmax_b, stats)
        l0, l1 = stats[0][1], stats[1][1]

        ot = acc[0] / l0 - lam * (acc[1] / l1)
        ms = jnp.mean(ot * ot, axis=0, keepdims=True)
        o = (ot * lax.rsqrt(ms + NORM_EPS)).T * g_ref[...]
        o_ref[rows, :] = (o * (1.0 - lam_init)).astype(o_ref.dtype)


def _attention(qk, v, lambda_qk, g_subln, *, bsz, seq, d, lam_init):
    heads = d // (2 * HEAD_DIM)
    hw = 2 * HEAD_DIM
    tqs = min(512, seq)
    n_sub = 2 if seq % (2 * tqs) == 0 else 1
    tq = n_sub * tqs
    tk = min(ATTN_KEY_CHUNK_SHORT, seq // 2) if seq <= ATTN_SHORT_SEQ else ATTN_KEY_CHUNK_LONG
    assert seq % (2 * tk) == 0
    nq = seq // tq
    kern = functools.partial(_attn_kernel, tk=tk, lam_init=lam_init)
    v_block_bytes = seq * hw * 2
    v_mode = pl.Buffered(1) if v_block_bytes > V_DOUBLE_BUFFER_MAX_BYTES else None
    return pl.pallas_call(
        kern,
        grid=(bsz, heads, nq),
        in_specs=[pl.BlockSpec((4, HEAD_DIM), lambda b, h, i: (0, 0)),
                  pl.BlockSpec((1, hw), lambda b, h, i: (0, 0)),
                  pl.BlockSpec((tq, hw), lambda b, h, i: (b * nq + i, h)),
                  pl.BlockSpec((seq, hw), lambda b, h, i: (b, heads + h)),
                  pl.BlockSpec((seq, hw), lambda b, h, i: (b, h), pipeline_mode=v_mode)],
        out_specs=pl.BlockSpec((tq, hw), lambda b, h, i: (b * nq + i, h)),
        out_shape=jax.ShapeDtypeStruct((bsz * seq, d), BF16),
        scratch_shapes=[pltpu.VMEM((seq // tk, hw, tk), BF16),
                        pltpu.VMEM((n_sub, 2, hw, tqs), F32),
                        pltpu.VMEM((2, tk, tqs), F32), pltpu.VMEM((2, tk, tqs), F32)],
        compiler_params=_params(3),
        name="diff_attention",
    )(lambda_qk, g_subln.reshape(1, hw), qk, qk, v)


def _conv_kernel(u_ref, prev_ref, next_ref, b_ref, w_ref, o_ref, *, tiles_per_seq):
    i = pl.program_id(0)
    ts = u_ref.shape[0]
    u = u_ref[...]
    pos = i % tiles_per_seq
    prev_row = jnp.where(pos == 0, 0.0, prev_ref[SUBLANES - 1:SUBLANES, :])
    next_row = jnp.where(pos == tiles_per_seq - 1, 0.0, next_ref[0:1, :])
    row = lax.broadcasted_iota(jnp.int32, u.shape, 0)
    u_prev = jnp.where(row == 0, prev_row, pltpu.roll(u, 1, 0))
    u_next = jnp.where(row == ts - 1, next_row, pltpu.roll(u, ts - 1, 0))
    w = w_ref[...]
    y = u_prev * w[0:1] + u * w[1:2] + u_next * w[2:3]
    o_ref[...] = (b_ref[...] * y).astype(o_ref.dtype)


def _gated_conv(u, b_gate, conv_w, seq):
    m, d = u.shape
    ts = min(512, seq)
    tc = min(1024, d)
    rb = ts // SUBLANES
    n_rb = m // SUBLANES
    kern = functools.partial(_conv_kernel, tiles_per_seq=seq // ts)
    return pl.pallas_call(
        kern,
        grid=(m // ts, d // tc),
        in_specs=[pl.BlockSpec((ts, tc), lambda i, j: (i, j)),
                  pl.BlockSpec((SUBLANES, tc), lambda i, j: (jnp.maximum(i * rb - 1, 0), j)),
                  pl.BlockSpec((SUBLANES, tc), lambda i, j: (jnp.minimum((i + 1) * rb, n_rb - 1), j)),
                  pl.BlockSpec((ts, tc), lambda i, j: (i, j)),
                  pl.BlockSpec((3, tc), lambda i, j: (0, j))],
        out_specs=pl.BlockSpec((ts, tc), lambda i, j: (i, j)),
        out_shape=jax.ShapeDtypeStruct((m, d), BF16),
        compiler_params=_params(2),
        name="gated_conv",
    )(u, u, u, b_gate, conv_w)


def _ffn(x, h, gate, wg, wu, wo, seq, side_casts=()):
    m, d = x.shape
    fp = wg.shape[1]
    tm = min(1024, seq)
    tf = min(256 if side_casts else 512, fp)
    n_steps = (m // tm) * (fp // tf)
    rides = [_side_cast_rows(w.shape[0], n_steps) is not None for w in side_casts]
    hidden, *riding = _matmul(
        [h], [wg, wu], [], [(jax.ShapeDtypeStruct((m, fp), BF16), _tile_spec(tm, tf))],
        _epi_swiglu, m=m, n=fp, kdim=d, tm=tm, tn=tf, tk=d,
        side_casts=[w for w, r in zip(side_casts, rides) if r], name="ffn_in")
    riding = iter(riding)
    cast = [next(riding) if r else w.astype(BF16) for w, r in zip(side_casts, rides)]
    tn = min(1024, d)
    tk = _largest_tile(fp, 3072)
    out, = _matmul(
        [hidden], [wo],
        [(x, _tile_spec(tm, tn)), (gate, _seq_vec_spec(tm, tn, seq))],
        [(jax.ShapeDtypeStruct((m, d), F32), _tile_spec(tm, tn))],
        functools.partial(_epi_residual, 0.5), m=m, n=d, kdim=fp, tm=tm, tn=tn, tk=tk,
        name="ffn_out")
    return out, cast


def _rope_tables(seq):
    inv_freq = ROPE_THETA ** (-jnp.arange(0, HEAD_DIM, 2, dtype=F32) / HEAD_DIM)
    ang = jnp.arange(seq, dtype=F32)[:, None] * inv_freq[None, :]
    cos = jnp.cos(ang)
    sin = jnp.sin(ang)
    return jnp.concatenate([cos, cos], axis=-1), jnp.concatenate([-sin, sin], axis=-1)


def _mixer(x, h, gate, w, bsz, seq, lam_init):
    m, d = x.shape
    w_in = w["w_in"]
    tm = min(1024, seq)

    tn = min(1024, d)
    cos, sin = _rope_tables(seq)
    rope_spec = pl.BlockSpec((tm, HEAD_DIM), lambda i, j, k: (i % (seq // tm), 0))
    qk, = _matmul(
        [h], [w_in], [(cos, rope_spec), (sin, rope_spec)],
        [(jax.ShapeDtypeStruct((m, 2 * d), BF16), _tile_spec(tm, tn))],
        functools.partial(_epi_rope, d // tn, LOG2_E / math.sqrt(HEAD_DIM)),
        m=m, n=2 * d, kdim=d, tm=tm, tn=tn, tk=d, name="proj_qk")
    v, = _matmul(
        [h], [w_in], [], [(jax.ShapeDtypeStruct((m, d), BF16), _tile_spec(tm, tn))],
        _epi_cast, m=m, n=d, kdim=d, tm=tm, tn=tn, tk=d, b_col_off=(2 * d // tn,), name="proj_v")
    gates, = _matmul(
        [h], [w_in], [], [(jax.ShapeDtypeStruct((m, 2 * d), F32), _tile_spec(tm, tn))],
        _epi_sigmoid, m=m, n=2 * d, kdim=d, tm=tm, tn=tn, tk=d, b_col_off=(6 * d // tn,),
        name="proj_gates")
    tmc = min(512, seq)
    tnc = min(512, d)
    nb = d // tnc
    b_gate, u = _matmul(
        [h], [w_in, w_in, w_in], [],
        [(jax.ShapeDtypeStruct((m, d), F32), _tile_spec(tmc, tnc)),
         (jax.ShapeDtypeStruct((m, d), F32), _tile_spec(tmc, tnc))],
        _epi_conv_operands, m=m, n=d, kdim=d, tm=tmc, tn=tnc, tk=d,
        b_col_off=(3 * nb, 4 * nb, 5 * nb), name="proj_conv")

    o = _attention(qk, v, w["lambda_qk"], w["g_subln"], bsz=bsz, seq=seq, d=d, lam_init=lam_init)
    z = _gated_conv(u, b_gate, w["conv_w"], seq)

    tmm = min(1024, seq)
    tnm = min(256, d)
    merged, = _matmul(
        [o, z], [w["w_branch_attn"], w["w_branch_conv"]],
        [(gates, _tile_spec(tmm, tnm)), (gates, _tile_spec(tmm, tnm, d // tnm))],
        [(jax.ShapeDtypeStruct((m, d), BF16), _tile_spec(tmm, tnm))],
        _epi_merge, m=m, n=d, kdim=d, tm=tmm, tn=tnm, tk=d, pair_a=(0, 1), name="branch_merge")

    out, = _matmul(
        [merged], [w["w_out"]],
        [(x, _tile_spec(tm, tn)), (gate, _seq_vec_spec(tm, tn, seq))],
        [(jax.ShapeDtypeStruct((m, d), F32), _tile_spec(tm, tn))],
        functools.partial(_epi_residual, 1.0), m=m, n=d, kdim=d, tm=tm, tn=tn, tk=d,
        name="out_proj")
    return out


def _encode(x3, mod, layers, g_final):
    bsz, seq, d = x3.shape
    x = x3.reshape(bsz * seq, d)
    for l, w in enumerate(layers):
        lam_init = 0.8 - 0.6 * math.exp(-0.3 * l)
        md = mod[l].reshape(bsz, N_SUBLAYERS, 3, 1, d)
        shift = [md[:, s, 0] for s in range(N_SUBLAYERS)]
        scale = [md[:, s, 1] for s in range(N_SUBLAYERS)]
        gate = [md[:, s, 2] for s in range(N_SUBLAYERS)]
        h = _norm_mod(x, w["g_norm"][0], scale[0], shift[0], seq)
        pending = [name for name in MIXER_WEIGHTS if w[name].dtype != BF16]
        x, cast = _ffn(x, h, gate[0], *w["ffn"][0], seq, side_casts=[w[name] for name in pending])
        w.update(zip(pending, cast))
        h = _norm_mod(x, w["g_norm"][1], scale[1], shift[1], seq)
        x = _mixer(x, h, gate[1], w, bsz, seq, lam_init)
        h = _norm_mod(x, w["g_norm"][2], scale[2], shift[2], seq)
        x, _ = _ffn(x, h, gate[2], *w["ffn"][1], seq)
    return _final_norm(x, g_final, seq).reshape(bsz, seq, d)


def kernel(x_prompt, x_sample, c_prompt, c_sample, w_mod, b_mod, g_norm, w_ffn_in, w_ffn_out, w_in,
           conv_w, lambda_qk, g_subln, w_branch_attn, w_branch_conv, w_out, g_final):
    depth = w_mod.shape[0]
    ffn_dim = w_ffn_out.shape[2]
    fp = _round_up(ffn_dim, FFN_PAD) if ffn_dim > FFN_PAD else ffn_dim
    bp, bs = c_prompt.shape[0], c_sample.shape[0]
    rows = _round_up(bp + bs, SUBLANES)
    c_all = jnp.pad(jnp.concatenate([c_prompt, c_sample], axis=0), ((0, rows - bp - bs), (0, 0)))

    layers, mod_p, mod_s = [], [], []
    for l in range(depth):
        mod = _modulation(c_all, w_mod[l], b_mod[l])
        mod_p.append(mod[:bp])
        mod_s.append(mod[bp:bp + bs])
        ffn = []
        for j in range(2):
            wg = _ffn_in_half(w_ffn_in, l, j, 0, ffn_dim, fp)
            wu = _ffn_in_half(w_ffn_in, l, j, 1, ffn_dim, fp)
            wo = _ffn_out_padded(w_ffn_out, l, j, fp)
            ffn.append((wg, wu, wo))
        layers.append(dict(
            ffn=ffn, g_norm=g_norm[l], w_in=w_in[l], conv_w=conv_w[l],
            lambda_qk=lambda_qk[l], g_subln=g_subln[l], w_branch_attn=w_branch_attn[l],
            w_branch_conv=w_branch_conv[l], w_out=w_out[l]))

    y_prompt = _encode(x_prompt, mod_p, layers, g_final)
    y_sample = _encode(x_sample, mod_s, layers, g_final)
    return (y_prompt, y_sample)
```

```python
import functools
import math

import jax
import jax.numpy as jnp
from jax import lax
from jax.experimental import pallas as pl
from jax.experimental.pallas import tpu as pltpu

HEAD_DIM = 128
NORM_EPS = 1e-6
ROPE_THETA = 10000.0
N_SUBLAYERS = 3
LANES = 128
SUBLANES = 8
BF16_SUBLANES = 16
VMEM_LIMIT_BYTES = 60 * 1024 * 1024
FFN_PAD = 1024
LOG2_E = 1.4426950408889634
ATTN_SHORT_SEQ = 4096
ATTN_KEY_CHUNK_SHORT = 1024
ATTN_KEY_CHUNK_LONG = 512
V_DOUBLE_BUFFER_MAX_BYTES = 4 * 1024 * 1024
MIXER_WEIGHTS = ("w_in", "w_branch_attn", "w_branch_conv", "w_out")

F32 = jnp.float32
BF16 = jnp.bfloat16


def _params(n_grid_dims, flags=None):
    return pltpu.CompilerParams(
        dimension_semantics=("arbitrary",) * n_grid_dims,
        vmem_limit_bytes=VMEM_LIMIT_BYTES,
        flags=flags,
    )


def _round_up(x, m):
    return (x + m - 1) // m * m


def _largest_tile(total, cap):
    best = LANES
    for t in range(LANES, min(total, cap) + 1, LANES):
        if total % t == 0:
            best = t
    return best


def _cast_pad_cols_kernel(x_ref, o_ref):
    n = x_ref.shape[1]
    o_ref[:, :n] = x_ref[...].astype(o_ref.dtype)
    if o_ref.shape[1] > n:
        o_ref[:, n:] = jnp.zeros((o_ref.shape[0], o_ref.shape[1] - n), o_ref.dtype)


def _cast_pad_rows_kernel(x_ref, o_ref):
    n = x_ref.shape[0]
    o_ref[:n, :] = x_ref[...].astype(o_ref.dtype)
    if o_ref.shape[0] > n:
        o_ref[n:, :] = jnp.zeros((o_ref.shape[0] - n, o_ref.shape[1]), o_ref.dtype)


def _ffn_in_half(w, l, j, half, f, fp):
    d = w.shape[2]
    tr = min(256, d)
    return pl.pallas_call(
        _cast_pad_cols_kernel,
        grid=(d // tr,),
        in_specs=[pl.BlockSpec((None, None, tr, f), lambda i: (l, j, i, half))],
        out_specs=pl.BlockSpec((tr, fp), lambda i: (i, 0)),
        out_shape=jax.ShapeDtypeStruct((d, fp), BF16),
        compiler_params=_params(1),
        name="cast_ffn_in",
    )(w)


def _ffn_out_padded(w, l, j, fp):
    f, d = w.shape[2:]
    tc = min(256, d)
    return pl.pallas_call(
        _cast_pad_rows_kernel,
        grid=(d // tc,),
        in_specs=[pl.BlockSpec((None, None, f, tc), lambda c: (l, j, 0, c))],
        out_specs=pl.BlockSpec((fp, tc), lambda c: (0, c)),
        out_shape=jax.ShapeDtypeStruct((fp, d), BF16),
        compiler_params=_params(1),
        name="cast_ffn_out",
    )(w)


def _mod_kernel(c_ref, w_ref, b_ref, o_ref):
    o_ref[...] = jnp.dot(c_ref[...].astype(BF16), w_ref[...].astype(BF16),
                         preferred_element_type=F32) + b_ref[...]


def _modulation(c_all, w_mod, b_mod):
    rows, d = c_all.shape
    n = w_mod.shape[1]
    tn = min(512, n)
    return pl.pallas_call(
        _mod_kernel,
        grid=(n // tn,),
        in_specs=[pl.BlockSpec((rows, d), lambda j: (0, 0)),
                  pl.BlockSpec((d, tn), lambda j: (0, j)),
                  pl.BlockSpec((1, tn), lambda j: (0, j))],
        out_specs=pl.BlockSpec((rows, tn), lambda j: (0, j)),
        out_shape=jax.ShapeDtypeStruct((rows, n), F32),
        compiler_params=_params(1),
        name="modulation",
    )(c_all, w_mod, b_mod.reshape(1, n))


def _norm_mod_kernel(x_ref, g_ref, scale_ref, shift_ref, o_ref):
    x = x_ref[...]
    ms = jnp.mean(x * x, axis=-1, keepdims=True)
    xn = (x * lax.rsqrt(ms + NORM_EPS)) * g_ref[...]
    o_ref[...] = (xn * (1.0 + scale_ref[...]) + shift_ref[...]).astype(o_ref.dtype)


def _norm_kernel(x_ref, g_ref, o_ref):
    x = x_ref[...]
    ms = jnp.mean(x * x, axis=-1, keepdims=True)
    o_ref[...] = ((x * lax.rsqrt(ms + NORM_EPS)) * g_ref[...]).astype(o_ref.dtype)


def _norm_mod(x, g, scale, shift, seq):
    m, d = x.shape
    tm = min(512, seq)
    vec = pl.BlockSpec((None, 1, d), lambda i: (i * tm // seq, 0, 0))
    return pl.pallas_call(
        _norm_mod_kernel,
        grid=(m // tm,),
        in_specs=[pl.BlockSpec((tm, d), lambda i: (i, 0)),
                  pl.BlockSpec((1, d), lambda i: (0, 0)), vec, vec],
        out_specs=pl.BlockSpec((tm, d), lambda i: (i, 0)),
        out_shape=jax.ShapeDtypeStruct((m, d), BF16),
        compiler_params=_params(1),
        name="norm_mod",
    )(x, g.reshape(1, d), scale, shift)


def _final_norm(x, g, seq):
    m, d = x.shape
    tm = min(512, seq)
    return pl.pallas_call(
        _norm_kernel,
        grid=(m // tm,),
        in_specs=[pl.BlockSpec((tm, d), lambda i: (i, 0)),
                  pl.BlockSpec((1, d), lambda i: (0, 0))],
        out_specs=pl.BlockSpec((tm, d), lambda i: (i, 0)),
        out_shape=jax.ShapeDtypeStruct((m, d), F32),
        compiler_params=_params(1),
        name="final_norm",
    )(x, g.reshape(1, d))


def _mm_body(*refs, n_a, n_b, n_e, n_s, n_o, nk, pair_a, epilogue):
    a = refs[:n_a]
    b = refs[n_a:n_a + n_b]
    e = refs[n_a + n_b:n_a + n_b + n_e]
    side_in = refs[n_a + n_b + n_e:n_a + n_b + n_e + n_s]
    n_in = n_a + n_b + n_e + n_s
    o = refs[n_in:n_in + n_o]
    side_out = refs[n_in + n_o:n_in + n_o + n_s]
    acc = refs[n_in + n_o + n_s:]
    ids = (pl.program_id(0), pl.program_id(1))

    for s_in, s_out in zip(side_in, side_out):
        s_out[...] = s_in[...].astype(s_out.dtype)

    def products():
        return [jnp.dot(a[pair_a[t]][...], b[t][...], preferred_element_type=F32)
                for t in range(n_b)]

    if nk == 1:
        epilogue(ids, products(), e, o)
        return
    k = pl.program_id(2)

    @pl.when(k == 0)
    def _():
        for t, r in enumerate(products()):
            acc[t][...] = r

    @pl.when(jnp.logical_and(k > 0, k < nk - 1))
    def _():
        for t, r in enumerate(products()):
            acc[t][...] += r

    @pl.when(k == nk - 1)
    def _():
        epilogue(ids, [acc[t][...] + r for t, r in enumerate(products())], e, o)


def _side_cast_rows(rows, n_steps):
    for tr in range(BF16_SUBLANES, rows + 1, BF16_SUBLANES):
        if rows % tr == 0 and rows // tr <= n_steps:
            return tr
    return None


def _matmul(a_arrs, b_arrs, extras, outs, epilogue, *, m, n, kdim, tm, tn, tk,
            pair_a=None, b_col_off=None, side_casts=(), name):
    n_b = len(b_arrs)
    pair_a = tuple(pair_a) if pair_a is not None else (0,) * n_b
    b_col_off = tuple(b_col_off) if b_col_off is not None else (0,) * n_b
    assert m % tm == 0 and n % tn == 0 and kdim % tk == 0
    nj, nk = n // tn, kdim // tk
    n_steps = (m // tm) * nj * nk
    in_specs = [pl.BlockSpec((tm, tk), lambda i, j, k: (i, k)) for _ in a_arrs]
    for off in b_col_off:
        in_specs.append(pl.BlockSpec((tk, tn), functools.partial(
            lambda i, j, k, off: (k, j + off), off=off)))
    in_specs += [s for _, s in extras]
    out_specs = [s for _, s in outs]
    out_shape = [sh for sh, _ in outs]
    side_specs = []
    for w in side_casts:
        rows, cols = w.shape
        tr = _side_cast_rows(rows, n_steps)
        assert tr is not None
        spec = pl.BlockSpec((tr, cols), functools.partial(
            lambda i, j, k, last: (jnp.minimum((i * nj + j) * nk + k, last), 0),
            last=rows // tr - 1))
        side_specs.append(spec)
        out_shape.append(jax.ShapeDtypeStruct((rows, cols), BF16))
    scratch = [pltpu.VMEM((tm, tn), F32) for _ in range(n_b)] if nk > 1 else []
    body = functools.partial(_mm_body, n_a=len(a_arrs), n_b=n_b, n_e=len(extras),
                             n_s=len(side_casts), n_o=len(outs), nk=nk, pair_a=pair_a,
                             epilogue=epilogue)
    res = pl.pallas_call(
        body,
        grid=(m // tm, nj, nk),
        in_specs=in_specs + side_specs,
        out_specs=out_specs + side_specs,
        out_shape=out_shape,
        scratch_shapes=scratch,
        compiler_params=_params(3),
        name=name,
    )(*a_arrs, *b_arrs, *[x for x, _ in extras], *side_casts)
    return res


def _tile_spec(tm, tn, col_off=0):
    return pl.BlockSpec((tm, tn), lambda i, j, k: (i, j + col_off))


def _seq_vec_spec(tm, tn, seq):
    return pl.BlockSpec((None, 1, tn), lambda i, j, k: (i * tm // seq, 0, j))


def _epi_swiglu(ids, r, e, o):
    g, u = r
    o[0][...] = ((g * jax.nn.sigmoid(g)) * u).astype(o[0].dtype)


def _epi_residual(coef, ids, r, e, o):
    x_ref, gate_ref = e
    gate = gate_ref[...] if coef == 1.0 else coef * gate_ref[...]
    o[0][...] = x_ref[...] + gate * r[0]


def _epi_rope(n_q_tiles, q_scale, ids, r, e, o):
    cos = e[0][...]
    sin = e[1][...]
    s = jnp.where(ids[1] < n_q_tiles, q_scale, 1.0).astype(F32)
    acc = r[0]
    for g in range(acc.shape[1] // HEAD_DIM):
        blk = acc[:, g * HEAD_DIM:(g + 1) * HEAD_DIM]
        rot = pltpu.roll(blk, HEAD_DIM // 2, 1)
        o[0][:, g * HEAD_DIM:(g + 1) * HEAD_DIM] = ((blk * cos + rot * sin) * s).astype(o[0].dtype)


def _epi_cast(ids, r, e, o):
    o[0][...] = r[0].astype(o[0].dtype)


def _epi_conv_operands(ids, r, e, o):
    o[0][...] = r[0]
    o[1][...] = r[1] * r[2]


def _epi_sigmoid(ids, r, e, o):
    o[0][...] = jax.nn.sigmoid(r[0])


def _epi_merge(ids, r, e, o):
    o[0][...] = (e[0][...] * r[0] + e[1][...] * r[1]).astype(o[0].dtype)


def _attn_kernel(lq_ref, g_ref, q_ref, k_ref, v_ref, o_ref, vt_ref, acc_ref, sa_ref, sb_ref, *,
                 tk, lam_init):
    n_sub, _, _, tq = acc_ref.shape
    seq = k_ref.shape[0]
    nk = seq // tk

    @pl.when(pl.program_id(2) == 0)
    def _():
        def tr(j, carry):
            start = pl.multiple_of(j * tk, tk)
            vt_ref[j] = v_ref[pl.ds(start, tk), :].astype(F32).T.astype(BF16)
            return carry
        lax.fori_loop(0, nk, tr, 0)

    lq = lq_ref[...]
    lam = (jnp.exp(jnp.sum(lq[0:1] * lq[1:2], axis=-1, keepdims=True))
           - jnp.exp(jnp.sum(lq[2:3] * lq[3:4], axis=-1, keepdims=True)) + lam_init)

    def scores(q, j, s_ref):
        start = pl.multiple_of(j * tk, tk)
        kblk = k_ref[pl.ds(start, tk), :]
        cmax = []
        for c in range(2):
            st = lax.dot_general(kblk[:, c * HEAD_DIM:(c + 1) * HEAD_DIM], q[c],
                                 (((1,), (1,)), ((), ())), preferred_element_type=F32)
            s_ref[c] = st
            cmax.append(jnp.max(st, axis=0, keepdims=True))
        return cmax

    def consume(acc, j, s_ref, cmax, stats):
        vt = vt_ref[j]
        out = []
        for c in range(2):
            m, l = stats[c]
            m_new = jnp.maximum(m, cmax[c])
            alpha = jnp.exp2(m - m_new)
            p = jnp.exp2(s_ref[c] - m_new)
            l_new = alpha * l + jnp.sum(p, axis=0, keepdims=True)
            acc[c] = acc[c] * alpha + jnp.dot(vt, p.astype(BF16), preferred_element_type=F32)
            out.append((m_new, l_new))
        return out

    for t in range(n_sub):
        rows = slice(t * tq, (t + 1) * tq)
        q = (q_ref[rows, :HEAD_DIM], q_ref[rows, HEAD_DIM:])
        acc = acc_ref.at[t]
        acc[...] = jnp.zeros(acc.shape, F32)

        def body(jj, carry, q=q, acc=acc):
            stats, cmax_a = carry
            ja = 2 * jj
            cmax_b = scores(q, ja + 1, sb_ref)
            stats = consume(acc, ja, sa_ref, cmax_a, stats)
            cmax_a = scores(q, ja + 2, sa_ref)
            stats = consume(acc, ja + 1, sb_ref, cmax_b, stats)
            return stats, cmax_a

        neg = jnp.full((1, tq), -jnp.inf, F32)
        zero = jnp.zeros((1, tq), F32)
        cmax_a = scores(q, 0, sa_ref)
        stats, cmax_a = lax.fori_loop(0, nk // 2 - 1, body, ([(neg, zero), (neg, zero)], cmax_a),
                                      unroll=4)
        cmax_b = scores(q, nk - 1, sb_ref)
        stats = consume(acc, nk - 2, sa_ref, cmax_a, stats)
        stats = consume(acc, nk - 1, sb_ref, cmax_b, stats)
        l0, l1 = stats[0][1], stats[1][1]

        ot = acc[0] / l0 - lam * (acc[1] / l1)
        ms = jnp.mean(ot * ot, axis=0, keepdims=True)
        o = (ot * lax.rsqrt(ms + NORM_EPS)).T * g_ref[...]
        o_ref[rows, :] = (o * (1.0 - lam_init)).astype(o_ref.dtype)


def _attention(qk, v, lambda_qk, g_subln, *, bsz, seq, d, lam_init):
    heads = d // (2 * HEAD_DIM)
    hw = 2 * HEAD_DIM
    tqs = min(512, seq)
    n_sub = next(n for n in ((4, 2, 1) if seq <= ATTN_SHORT_SEQ else (2, 1)) if seq % (n * tqs) == 0)
    tq = n_sub * tqs
    tk = min(ATTN_KEY_CHUNK_SHORT, seq // 2) if seq <= ATTN_SHORT_SEQ else ATTN_KEY_CHUNK_LONG
    assert seq % (2 * tk) == 0
    nq = seq // tq
    kern = functools.partial(_attn_kernel, tk=tk, lam_init=lam_init)
    v_block_bytes = seq * hw * 2
    v_mode = pl.Buffered(1) if v_block_bytes > V_DOUBLE_BUFFER_MAX_BYTES else None
    return pl.pallas_call(
        kern,
        grid=(bsz, heads, nq),
        in_specs=[pl.BlockSpec((4, HEAD_DIM), lambda b, h, i: (0, 0)),
                  pl.BlockSpec((1, hw), lambda b, h, i: (0, 0)),
                  pl.BlockSpec((tq, hw), lambda b, h, i: (b * nq + i, h)),
                  pl.BlockSpec((seq, hw), lambda b, h, i: (b, heads + h)),
                  pl.BlockSpec((seq, hw), lambda b, h, i: (b, h), pipeline_mode=v_mode)],
        out_specs=pl.BlockSpec((tq, hw), lambda b, h, i: (b * nq + i, h)),
        out_shape=jax.ShapeDtypeStruct((bsz * seq, d), BF16),
        scratch_shapes=[pltpu.VMEM((seq // tk, hw, tk), BF16),
                        pltpu.VMEM((n_sub, 2, hw, tqs), F32),
                        pltpu.VMEM((2, tk, tqs), F32), pltpu.VMEM((2, tk, tqs), F32)],
        compiler_params=_params(3),
        name="diff_attention",
    )(lambda_qk, g_subln.reshape(1, hw), qk, qk, v)


def _conv_kernel(u_ref, prev_ref, next_ref, b_ref, w_ref, o_ref, *, tiles_per_seq):
    i = pl.program_id(0)
    ts = u_ref.shape[0]
    u = u_ref[...]
    pos = i % tiles_per_seq
    prev_row = jnp.where(pos == 0, 0.0, prev_ref[SUBLANES - 1:SUBLANES, :])
    next_row = jnp.where(pos == tiles_per_seq - 1, 0.0, next_ref[0:1, :])
    row = lax.broadcasted_iota(jnp.int32, u.shape, 0)
    u_prev = jnp.where(row == 0, prev_row, pltpu.roll(u, 1, 0))
    u_next = jnp.where(row == ts - 1, next_row, pltpu.roll(u, ts - 1, 0))
    w = w_ref[...]
    y = u_prev * w[0:1] + u * w[1:2] + u_next * w[2:3]
    o_ref[...] = (b_ref[...] * y).astype(o_ref.dtype)


def _gated_conv(u, b_gate, conv_w, seq):
    m, d = u.shape
    ts = min(512, seq)
    tc = min(2048, d)
    rb = ts // SUBLANES
    n_rb = m // SUBLANES
    kern = functools.partial(_conv_kernel, tiles_per_seq=seq // ts)
    return pl.pallas_call(
        kern,
        grid=(m // ts, d // tc),
        in_specs=[pl.BlockSpec((ts, tc), lambda i, j: (i, j)),
                  pl.BlockSpec((SUBLANES, tc), lambda i, j: (jnp.maximum(i * rb - 1, 0), j)),
                  pl.BlockSpec((SUBLANES, tc), lambda i, j: (jnp.minimum((i + 1) * rb, n_rb - 1), j)),
                  pl.BlockSpec((ts, tc), lambda i, j: (i, j)),
                  pl.BlockSpec((3, tc), lambda i, j: (0, j))],
        out_specs=pl.BlockSpec((ts, tc), lambda i, j: (i, j)),
        out_shape=jax.ShapeDtypeStruct((m, d), BF16),
        compiler_params=_params(2),
        name="gated_conv",
    )(u, u, u, b_gate, conv_w)


def _ffn(x, h, gate, wg, wu, wo, seq, side_casts=()):
    m, d = x.shape
    fp = wg.shape[1]
    tm = min(1024, seq)
    tf = min(512, fp)
    tn = min(1024, d)
    tk = _largest_tile(fp, 3072)
    order = sorted(range(len(side_casts)), key=lambda t: -side_casts[t].size)
    steps = {"in": (m // tm) * (fp // tf), "out": (m // tm) * (d // tn) * (fp // tk)}
    host = {t: ("in" if t == order[0] else "out") for t in order}
    host = {t: hst if _side_cast_rows(side_casts[t].shape[0], steps[hst]) is not None else None
            for t, hst in host.items()}
    riders = {hst: [t for t in order if host[t] == hst] for hst in ("in", "out")}
    cast = {t: side_casts[t].astype(BF16) for t in order if host[t] is None}

    hidden, *riding = _matmul(
        [h], [wg, wu], [], [(jax.ShapeDtypeStruct((m, fp), BF16), _tile_spec(tm, tf))],
        _epi_swiglu, m=m, n=fp, kdim=d, tm=tm, tn=tf, tk=d,
        side_casts=[side_casts[t] for t in riders["in"]], name="ffn_in")
    cast.update(zip(riders["in"], riding))
    out, *riding = _matmul(
        [hidden], [wo],
        [(x, _tile_spec(tm, tn)), (gate, _seq_vec_spec(tm, tn, seq))],
        [(jax.ShapeDtypeStruct((m, d), F32), _tile_spec(tm, tn))],
        functools.partial(_epi_residual, 0.5), m=m, n=d, kdim=fp, tm=tm, tn=tn, tk=tk,
        side_casts=[side_casts[t] for t in riders["out"]], name="ffn_out")
    cast.update(zip(riders["out"], riding))
    return out, [cast[t] for t in range(len(side_casts))]


def _rope_tables(seq):
    inv_freq = ROPE_THETA ** (-jnp.arange(0, HEAD_DIM, 2, dtype=F32) / HEAD_DIM)
    ang = jnp.arange(seq, dtype=F32)[:, None] * inv_freq[None, :]
    cos = jnp.cos(ang)
    sin = jnp.sin(ang)
    return jnp.concatenate([cos, cos], axis=-1), jnp.concatenate([-sin, sin], axis=-1)


def _mixer(x, h, gate, w, bsz, seq, lam_init):
    m, d = x.shape
    w_in = w["w_in"]
    tm = min(1024, seq)

    tn = min(1024, d)
    cos, sin = _rope_tables(seq)
    rope_spec = pl.BlockSpec((tm, HEAD_DIM), lambda i, j, k: (i % (seq // tm), 0))
    qk, = _matmul(
        [h], [w_in], [(cos, rope_spec), (sin, rope_spec)],
        [(jax.ShapeDtypeStruct((m, 2 * d), BF16), _tile_spec(tm, tn))],
        functools.partial(_epi_rope, d // tn, LOG2_E / math.sqrt(HEAD_DIM)),
        m=m, n=2 * d, kdim=d, tm=tm, tn=tn, tk=d, name="proj_qk")
    v, = _matmul(
        [h], [w_in], [], [(jax.ShapeDtypeStruct((m, d), BF16), _tile_spec(tm, tn))],
        _epi_cast, m=m, n=d, kdim=d, tm=tm, tn=tn, tk=d, b_col_off=(2 * d // tn,), name="proj_v")
    gates, = _matmul(
        [h], [w_in], [], [(jax.ShapeDtypeStruct((m, 2 * d), F32), _tile_spec(tm, tn))],
        _epi_sigmoid, m=m, n=2 * d, kdim=d, tm=tm, tn=tn, tk=d, b_col_off=(6 * d // tn,),
        name="proj_gates")
    tmc = min(512, seq)
    tnc = min(512, d)
    nb = d // tnc
    b_gate, u = _matmul(
        [h], [w_in, w_in, w_in], [],
        [(jax.ShapeDtypeStruct((m, d), F32), _tile_spec(tmc, tnc)),
         (jax.ShapeDtypeStruct((m, d), F32), _tile_spec(tmc, tnc))],
        _epi_conv_operands, m=m, n=d, kdim=d, tm=tmc, tn=tnc, tk=d,
        b_col_off=(3 * nb, 4 * nb, 5 * nb), name="proj_conv")

    o = _attention(qk, v, w["lambda_qk"], w["g_subln"], bsz=bsz, seq=seq, d=d, lam_init=lam_init)
    z = _gated_conv(u, b_gate, w["conv_w"], seq)

    tmm = min(1024, seq)
    tnm = min(256, d)
    merged, = _matmul(
        [o, z], [w["w_branch_attn"], w["w_branch_conv"]],
        [(gates, _tile_spec(tmm, tnm)), (gates, _tile_spec(tmm, tnm, d // tnm))],
        [(jax.ShapeDtypeStruct((m, d), BF16), _tile_spec(tmm, tnm))],
        _epi_merge, m=m, n=d, kdim=d, tm=tmm, tn=tnm, tk=d, pair_a=(0, 1), name="branch_merge")

    out, = _matmul(
        [merged], [w["w_out"]],
        [(x, _tile_spec(tm, tn)), (gate, _seq_vec_spec(tm, tn, seq))],
        [(jax.ShapeDtypeStruct((m, d), F32), _tile_spec(tm, tn))],
        functools.partial(_epi_residual, 1.0), m=m, n=d, kdim=d, tm=tm, tn=tn, tk=d,
        name="out_proj")
    return out


def _encode(x3, mod, layers, g_final):
    bsz, seq, d = x3.shape
    x = x3.reshape(bsz * seq, d)
    for l, w in enumerate(layers):
        lam_init = 0.8 - 0.6 * math.exp(-0.3 * l)
        md = mod[l].reshape(bsz, N_SUBLAYERS, 3, 1, d)
        shift = [md[:, s, 0] for s in range(N_SUBLAYERS)]
        scale = [md[:, s, 1] for s in range(N_SUBLAYERS)]
        gate = [md[:, s, 2] for s in range(N_SUBLAYERS)]
        h = _norm_mod(x, w["g_norm"][0], scale[0], shift[0], seq)
        pending = [name for name in MIXER_WEIGHTS if w[name].dtype != BF16]
        x, cast = _ffn(x, h, gate[0], *w["ffn"][0], seq, side_casts=[w[name] for name in pending])
        w.update(zip(pending, cast))
        h = _norm_mod(x, w["g_norm"][1], scale[1], shift[1], seq)
        x = _mixer(x, h, gate[1], w, bsz, seq, lam_init)
        h = _norm_mod(x, w["g_norm"][2], scale[2], shift[2], seq)
        x, _ = _ffn(x, h, gate[2], *w["ffn"][1], seq)
    return _final_norm(x, g_final, seq).reshape(bsz, seq, d)


def kernel(x_prompt, x_sample, c_prompt, c_sample, w_mod, b_mod, g_norm, w_ffn_in, w_ffn_out, w_in,
           conv_w, lambda_qk, g_subln, w_branch_attn, w_branch_conv, w_out, g_final):
    depth = w_mod.shape[0]
    ffn_dim = w_ffn_out.shape[2]
    fp = _round_up(ffn_dim, FFN_PAD) if ffn_dim > FFN_PAD else ffn_dim
    bp, bs = c_prompt.shape[0], c_sample.shape[0]
    rows = _round_up(bp + bs, SUBLANES)
    c_all = jnp.pad(jnp.concatenate([c_prompt, c_sample], axis=0), ((0, rows - bp - bs), (0, 0)))

    layers, mod_p, mod_s = [], [], []
    for l in range(depth):
        mod = _modulation(c_all, w_mod[l], b_mod[l])
        mod_p.append(mod[:bp])
        mod_s.append(mod[bp:bp + bs])
        ffn = []
        for j in range(2):
            wg = _ffn_in_half(w_ffn_in, l, j, 0, ffn_dim, fp)
            wu = _ffn_in_half(w_ffn_in, l, j, 1, ffn_dim, fp)
            wo = _ffn_out_padded(w_ffn_out, l, j, fp)
            ffn.append((wg, wu, wo))
        layers.append(dict(
            ffn=ffn, g_norm=g_norm[l], w_in=w_in[l], conv_w=conv_w[l],
            lambda_qk=lambda_qk[l], g_subln=g_subln[l], w_branch_attn=w_branch_attn[l],
            w_branch_conv=w_branch_conv[l], w_out=w_out[l]))

    y_prompt = _encode(x_prompt, mod_p, layers, g_final)
    y_sample = _encode(x_sample, mod_s, layers, g_final)
    return (y_prompt, y_sample)
```

```python
import functools
import math
from typing import Any, Callable, NamedTuple

import jax
import jax.numpy as jnp
from jax import lax
from jax.experimental import pallas as pl
from jax.experimental.pallas import tpu as pltpu

HEAD_DIM = 128
NORM_EPS = 1e-6
ROPE_THETA = 10000.0
N_SUBLAYERS = 3
LANES = 128
SUBLANES = 8
BF16_SUBLANES = 16
VMEM_LIMIT_BYTES = 60 * 1024 * 1024
FFN_PAD = 1024
LOG2_E = 1.4426950408889634
ATTN_SHORT_SEQ = 4096
ATTN_KEY_CHUNK_SHORT = 1024
ATTN_KEY_CHUNK_LONG = 512
V_DOUBLE_BUFFER_MAX_BYTES = 4 * 1024 * 1024
MIXER_WEIGHTS = ("w_in", "w_branch_attn", "w_branch_conv", "w_out")

F32 = jnp.float32
BF16 = jnp.bfloat16


def _params(n_grid_dims, flags=None):
    return pltpu.CompilerParams(
        dimension_semantics=("arbitrary",) * n_grid_dims,
        vmem_limit_bytes=VMEM_LIMIT_BYTES,
        flags=flags,
    )


def _round_up(x, m):
    return (x + m - 1) // m * m


def _largest_tile(total, cap):
    best = LANES
    for t in range(LANES, min(total, cap) + 1, LANES):
        if total % t == 0:
            best = t
    return best


def _cast_pad_cols_kernel(x_ref, o_ref):
    n = x_ref.shape[1]
    o_ref[:, :n] = x_ref[...].astype(o_ref.dtype)
    if o_ref.shape[1] > n:
        o_ref[:, n:] = jnp.zeros((o_ref.shape[0], o_ref.shape[1] - n), o_ref.dtype)


def _cast_pad_rows_kernel(x_ref, o_ref):
    n = x_ref.shape[0]
    o_ref[:n, :] = x_ref[...].astype(o_ref.dtype)
    if o_ref.shape[0] > n:
        o_ref[n:, :] = jnp.zeros((o_ref.shape[0] - n, o_ref.shape[1]), o_ref.dtype)


def _ffn_in_half(w, l, j, half, f, fp):
    d = w.shape[2]
    tr = min(256, d)
    return pl.pallas_call(
        _cast_pad_cols_kernel,
        grid=(d // tr,),
        in_specs=[pl.BlockSpec((None, None, tr, f), lambda i: (l, j, i, half))],
        out_specs=pl.BlockSpec((tr, fp), lambda i: (i, 0)),
        out_shape=jax.ShapeDtypeStruct((d, fp), BF16),
        compiler_params=_params(1),
        name="cast_ffn_in",
    )(w)


def _ffn_out_padded(w, l, j, fp):
    f, d = w.shape[2:]
    tc = min(256, d)
    return pl.pallas_call(
        _cast_pad_rows_kernel,
        grid=(d // tc,),
        in_specs=[pl.BlockSpec((None, None, f, tc), lambda c: (l, j, 0, c))],
        out_specs=pl.BlockSpec((fp, tc), lambda c: (0, c)),
        out_shape=jax.ShapeDtypeStruct((fp, d), BF16),
        compiler_params=_params(1),
        name="cast_ffn_out",
    )(w)


def _mod_kernel(c_ref, w_ref, b_ref, o_ref):
    o_ref[...] = jnp.dot(c_ref[...].astype(BF16), w_ref[...].astype(BF16),
                         preferred_element_type=F32) + b_ref[...]


def _modulation(c_all, w_mod, b_mod):
    rows, d = c_all.shape
    n = w_mod.shape[1]
    tn = min(512, n)
    return pl.pallas_call(
        _mod_kernel,
        grid=(n // tn,),
        in_specs=[pl.BlockSpec((rows, d), lambda j: (0, 0)),
                  pl.BlockSpec((d, tn), lambda j: (0, j)),
                  pl.BlockSpec((1, tn), lambda j: (0, j))],
        out_specs=pl.BlockSpec((rows, tn), lambda j: (0, j)),
        out_shape=jax.ShapeDtypeStruct((rows, n), F32),
        compiler_params=_params(1),
        name="modulation",
    )(c_all, w_mod, b_mod.reshape(1, n))


def _norm_mod_kernel(x_ref, g_ref, scale_ref, shift_ref, o_ref):
    x = x_ref[...]
    ms = jnp.mean(x * x, axis=-1, keepdims=True)
    xn = (x * lax.rsqrt(ms + NORM_EPS)) * g_ref[...]
    o_ref[...] = (xn * (1.0 + scale_ref[...]) + shift_ref[...]).astype(o_ref.dtype)


def _norm_kernel(x_ref, g_ref, o_ref):
    x = x_ref[...]
    ms = jnp.mean(x * x, axis=-1, keepdims=True)
    o_ref[...] = ((x * lax.rsqrt(ms + NORM_EPS)) * g_ref[...]).astype(o_ref.dtype)


def _norm_mod(x, g, scale, shift, seq):
    m, d = x.shape
    tm = min(512, seq)
    vec = pl.BlockSpec((None, 1, d), lambda i: (i * tm // seq, 0, 0))
    return pl.pallas_call(
        _norm_mod_kernel,
        grid=(m // tm,),
        in_specs=[pl.BlockSpec((tm, d), lambda i: (i, 0)),
                  pl.BlockSpec((1, d), lambda i: (0, 0)), vec, vec],
        out_specs=pl.BlockSpec((tm, d), lambda i: (i, 0)),
        out_shape=jax.ShapeDtypeStruct((m, d), BF16),
        compiler_params=_params(1),
        name="norm_mod",
    )(x, g.reshape(1, d), scale, shift)


def _final_norm(x, g, seq):
    m, d = x.shape
    tm = min(512, seq)
    return pl.pallas_call(
        _norm_kernel,
        grid=(m // tm,),
        in_specs=[pl.BlockSpec((tm, d), lambda i: (i, 0)),
                  pl.BlockSpec((1, d), lambda i: (0, 0))],
        out_specs=pl.BlockSpec((tm, d), lambda i: (i, 0)),
        out_shape=jax.ShapeDtypeStruct((m, d), F32),
        compiler_params=_params(1),
        name="final_norm",
    )(x, g.reshape(1, d))


def _mm_body(*refs, n_a, n_b, n_e, side_valid, n_o, nj, nk, pair_a, epilogue):
    n_s = len(side_valid)
    a = refs[:n_a]
    b = refs[n_a:n_a + n_b]
    e = refs[n_a + n_b:n_a + n_b + n_e]
    side_in = refs[n_a + n_b + n_e:n_a + n_b + n_e + n_s]
    n_in = n_a + n_b + n_e + n_s
    o = refs[n_in:n_in + n_o]
    side_out = refs[n_in + n_o:n_in + n_o + n_s]
    acc = refs[n_in + n_o + n_s:]
    ids = (pl.program_id(0), pl.program_id(1))

    step = (ids[0] * nj + ids[1]) * nk + (pl.program_id(2) if nk > 1 else 0)
    for s_in, s_out, n_valid in zip(side_in, side_out, side_valid):
        val = s_in[...].astype(s_out.dtype)
        if n_valid is not None:
            val = jnp.where(step < n_valid, val, jnp.zeros_like(val))
        cols = s_in.shape[-1]
        s_out[:, :cols] = val
        if s_out.shape[-1] > cols:
            s_out[:, cols:] = jnp.zeros((s_out.shape[0], s_out.shape[-1] - cols), s_out.dtype)

    def products():
        return [jnp.dot(a[pair_a[t]][...], b[t][...], preferred_element_type=F32)
                for t in range(n_b)]

    if nk == 1:
        epilogue(ids, products(), e, o)
        return
    k = pl.program_id(2)

    @pl.when(k == 0)
    def _():
        for t, r in enumerate(products()):
            acc[t][...] = r

    @pl.when(jnp.logical_and(k > 0, k < nk - 1))
    def _():
        for t, r in enumerate(products()):
            acc[t][...] += r

    @pl.when(k == nk - 1)
    def _():
        epilogue(ids, [acc[t][...] + r for t, r in enumerate(products())], e, o)


class CastJob(NamedTuple):
    x: Any
    in_block: tuple
    in_index: Callable
    out_shape: tuple
    n_blocks: int
    n_valid: int


def _side_cast_rows(rows, n_steps, also_divides=None):
    for tr in range(BF16_SUBLANES, rows + 1, BF16_SUBLANES):
        if rows % tr == 0 and rows // tr <= n_steps and (also_divides or rows) % tr == 0:
            return tr
    return None


def _plain_cast_job(w, n_steps):
    rows, cols = w.shape
    tr = _side_cast_rows(rows, n_steps)
    if tr is None:
        return None
    return CastJob(w, (tr, cols), lambda b: (b, 0), (rows, cols), rows // tr, rows // tr)


def _ffn_in_half_job(w, l, j, half, f, fp, n_steps):
    d = w.shape[2]
    tr = _side_cast_rows(d, n_steps)
    if tr is None:
        return None
    return CastJob(w, (None, None, tr, f), lambda b: (l, j, b, half), (d, fp), d // tr, d // tr)


def _ffn_out_job(w, l, j, fp, n_steps):
    f, d = w.shape[2:]
    tr = _side_cast_rows(fp, n_steps, also_divides=f)
    if tr is None:
        return None
    n_valid = f // tr
    return CastJob(w, (None, None, tr, d), lambda b: (l, j, jnp.minimum(b, n_valid - 1), 0),
                   (fp, d), fp // tr, n_valid)


def _matmul(a_arrs, b_arrs, extras, outs, epilogue, *, m, n, kdim, tm, tn, tk,
            pair_a=None, b_col_off=None, side_casts=(), name):
    n_b = len(b_arrs)
    pair_a = tuple(pair_a) if pair_a is not None else (0,) * n_b
    b_col_off = tuple(b_col_off) if b_col_off is not None else (0,) * n_b
    assert m % tm == 0 and n % tn == 0 and kdim % tk == 0
    nj, nk = n // tn, kdim // tk
    n_steps = (m // tm) * nj * nk
    in_specs = [pl.BlockSpec((tm, tk), lambda i, j, k: (i, k)) for _ in a_arrs]
    for off in b_col_off:
        in_specs.append(pl.BlockSpec((tk, tn), functools.partial(
            lambda i, j, k, off: (k, j + off), off=off)))
    in_specs += [s for _, s in extras]
    out_specs = [s for _, s in outs]
    out_shape = [sh for sh, _ in outs]
    side_in_specs, side_out_specs = [], []
    for job in side_casts:
        assert job.n_blocks <= n_steps
        block = functools.partial(
            lambda i, j, k, last: jnp.minimum((i * nj + j) * nk + k, last), last=job.n_blocks - 1)
        side_in_specs.append(pl.BlockSpec(
            job.in_block, functools.partial(lambda i, j, k, job, block: job.in_index(block(i, j, k)),
                                            job=job, block=block)))
        side_out_specs.append(pl.BlockSpec(
            (job.in_block[-2], job.out_shape[1]),
            functools.partial(lambda i, j, k, block: (block(i, j, k), 0), block=block)))
        out_shape.append(jax.ShapeDtypeStruct(job.out_shape, BF16))
    side_valid = tuple(None if job.n_valid == job.n_blocks else job.n_valid for job in side_casts)
    scratch = [pltpu.VMEM((tm, tn), F32) for _ in range(n_b)] if nk > 1 else []
    body = functools.partial(_mm_body, n_a=len(a_arrs), n_b=n_b, n_e=len(extras),
                             side_valid=side_valid, n_o=len(outs), nj=nj, nk=nk, pair_a=pair_a,
                             epilogue=epilogue)
    res = pl.pallas_call(
        body,
        grid=(m // tm, nj, nk),
        in_specs=in_specs + side_in_specs,
        out_specs=out_specs + side_out_specs,
        out_shape=out_shape,
        scratch_shapes=scratch,
        compiler_params=_params(3),
        name=name,
    )(*a_arrs, *b_arrs, *[x for x, _ in extras], *[job.x for job in side_casts])
    return res


def _tile_spec(tm, tn, col_off=0):
    return pl.BlockSpec((tm, tn), lambda i, j, k: (i, j + col_off))


def _seq_vec_spec(tm, tn, seq):
    return pl.BlockSpec((None, 1, tn), lambda i, j, k: (i * tm // seq, 0, j))


def _epi_swiglu(ids, r, e, o):
    g, u = r
    o[0][...] = ((g * jax.nn.sigmoid(g)) * u).astype(o[0].dtype)


def _epi_residual(coef, ids, r, e, o):
    x_ref, gate_ref = e
    gate = gate_ref[...] if coef == 1.0 else coef * gate_ref[...]
    o[0][...] = x_ref[...] + gate * r[0]


def _epi_rope(n_q_tiles, q_scale, ids, r, e, o):
    cos = e[0][...]
    sin = e[1][...]
    s = jnp.where(ids[1] < n_q_tiles, q_scale, 1.0).astype(F32)
    acc = r[0]
    for g in range(acc.shape[1] // HEAD_DIM):
        blk = acc[:, g * HEAD_DIM:(g + 1) * HEAD_DIM]
        rot = pltpu.roll(blk, HEAD_DIM // 2, 1)
        o[0][:, g * HEAD_DIM:(g + 1) * HEAD_DIM] = ((blk * cos + rot * sin) * s).astype(o[0].dtype)


def _epi_cast(ids, r, e, o):
    o[0][...] = r[0].astype(o[0].dtype)


def _epi_conv_operands(ids, r, e, o):
    o[0][...] = r[0]
    o[1][...] = r[1] * r[2]


def _epi_sigmoid(ids, r, e, o):
    o[0][...] = jax.nn.sigmoid(r[0])


def _epi_merge(ids, r, e, o):
    o[0][...] = (e[0][...] * r[0] + e[1][...] * r[1]).astype(o[0].dtype)


def _attn_kernel(lq_ref, g_ref, q_ref, k_ref, v_ref, o_ref, vt_ref, acc_ref, sa_ref, sb_ref, *,
                 tk, lam_init):
    n_sub, _, _, tq = acc_ref.shape
    seq = k_ref.shape[0]
    nk = seq // tk

    @pl.when(pl.program_id(2) == 0)
    def _():
        def tr(j, carry):
            start = pl.multiple_of(j * tk, tk)
            vt_ref[j] = v_ref[pl.ds(start, tk), :].astype(F32).T.astype(BF16)
            return carry
        lax.fori_loop(0, nk, tr, 0)

    lq = lq_ref[...]
    lam = (jnp.exp(jnp.sum(lq[0:1] * lq[1:2], axis=-1, keepdims=True))
           - jnp.exp(jnp.sum(lq[2:3] * lq[3:4], axis=-1, keepdims=True)) + lam_init)

    def scores(q, j, s_ref):
        start = pl.multiple_of(j * tk, tk)
        kblk = k_ref[pl.ds(start, tk), :]
        cmax = []
        for c in range(2):
            st = lax.dot_general(kblk[:, c * HEAD_DIM:(c + 1) * HEAD_DIM], q[c],
                                 (((1,), (1,)), ((), ())), preferred_element_type=F32)
            s_ref[c] = st
            cmax.append(jnp.max(st, axis=0, keepdims=True))
        return cmax

    def consume(acc, j, s_ref, cmax, stats):
        vt = vt_ref[j]
        out = []
        for c in range(2):
            m, l = stats[c]
            m_new = jnp.maximum(m, cmax[c])
            alpha = jnp.exp2(m - m_new)
            p = jnp.exp2(s_ref[c] - m_new)
            l_new = alpha * l + jnp.sum(p, axis=0, keepdims=True)
            acc[c] = acc[c] * alpha + jnp.dot(vt, p.astype(BF16), preferred_element_type=F32)
            out.append((m_new, l_new))
        return out

    for t in range(n_sub):
        rows = slice(t * tq, (t + 1) * tq)
        q = (q_ref[rows, :HEAD_DIM], q_ref[rows, HEAD_DIM:])
        acc = acc_ref.at[t]
        acc[...] = jnp.zeros(acc.shape, F32)

        def body(jj, carry, q=q, acc=acc):
            stats, cmax_a = carry
            ja = 2 * jj
            cmax_b = scores(q, ja + 1, sb_ref)
            stats = consume(acc, ja, sa_ref, cmax_a, stats)
            cmax_a = scores(q, ja + 2, sa_ref)
            stats = consume(acc, ja + 1, sb_ref, cmax_b, stats)
            return stats, cmax_a

        neg = jnp.full((1, tq), -jnp.inf, F32)
        zero = jnp.zeros((1, tq), F32)
        cmax_a = scores(q, 0, sa_ref)
        stats, cmax_a = lax.fori_loop(0, nk // 2 - 1, body, ([(neg, zero), (neg, zero)], cmax_a),
                                      unroll=4)
        cmax_b = scores(q, nk - 1, sb_ref)
        stats = consume(acc, nk - 2, sa_ref, cmax_a, stats)
        stats = consume(acc, nk - 1, sb_ref, cmax_b, stats)
        l0, l1 = stats[0][1], stats[1][1]

        ot = acc[0] / l0 - lam * (acc[1] / l1)
        ms = jnp.mean(ot * ot, axis=0, keepdims=True)
        o = (ot * lax.rsqrt(ms + NORM_EPS)).T * g_ref[...]
        o_ref[rows, :] = (o * (1.0 - lam_init)).astype(o_ref.dtype)


def _attention(qk, v, lambda_qk, g_subln, *, bsz, seq, d, lam_init):
    heads = d // (2 * HEAD_DIM)
    hw = 2 * HEAD_DIM
    tqs = min(512, seq)
    n_sub = next(n for n in ((4, 2, 1) if seq <= ATTN_SHORT_SEQ else (2, 1)) if seq % (n * tqs) == 0)
    tq = n_sub * tqs
    tk = min(ATTN_KEY_CHUNK_SHORT, seq // 2) if seq <= ATTN_SHORT_SEQ else ATTN_KEY_CHUNK_LONG
    assert seq % (2 * tk) == 0
    nq = seq // tq
    kern = functools.partial(_attn_kernel, tk=tk, lam_init=lam_init)
    v_block_bytes = seq * hw * 2
    v_mode = pl.Buffered(1) if v_block_bytes > V_DOUBLE_BUFFER_MAX_BYTES else None
    return pl.pallas_call(
        kern,
        grid=(bsz, heads, nq),
        in_specs=[pl.BlockSpec((4, HEAD_DIM), lambda b, h, i: (0, 0)),
                  pl.BlockSpec((1, hw), lambda b, h, i: (0, 0)),
                  pl.BlockSpec((tq, hw), lambda b, h, i: (b * nq + i, h)),
                  pl.BlockSpec((seq, hw), lambda b, h, i: (b, heads + h)),
                  pl.BlockSpec((seq, hw), lambda b, h, i: (b, h), pipeline_mode=v_mode)],
        out_specs=pl.BlockSpec((tq, hw), lambda b, h, i: (b * nq + i, h)),
        out_shape=jax.ShapeDtypeStruct((bsz * seq, d), BF16),
        scratch_shapes=[pltpu.VMEM((seq // tk, hw, tk), BF16),
                        pltpu.VMEM((n_sub, 2, hw, tqs), F32),
                        pltpu.VMEM((2, tk, tqs), F32), pltpu.VMEM((2, tk, tqs), F32)],
        compiler_params=_params(3),
        name="diff_attention",
    )(lambda_qk, g_subln.reshape(1, hw), qk, qk, v)


def _conv_kernel(u_ref, prev_ref, next_ref, b_ref, w_ref, o_ref, *, tiles_per_seq):
    i = pl.program_id(0)
    ts = u_ref.shape[0]
    u = u_ref[...]
    pos = i % tiles_per_seq
    prev_row = jnp.where(pos == 0, 0.0, prev_ref[SUBLANES - 1:SUBLANES, :])
    next_row = jnp.where(pos == tiles_per_seq - 1, 0.0, next_ref[0:1, :])
    row = lax.broadcasted_iota(jnp.int32, u.shape, 0)
    u_prev = jnp.where(row == 0, prev_row, pltpu.roll(u, 1, 0))
    u_next = jnp.where(row == ts - 1, next_row, pltpu.roll(u, ts - 1, 0))
    w = w_ref[...]
    y = u_prev * w[0:1] + u * w[1:2] + u_next * w[2:3]
    o_ref[...] = (b_ref[...] * y).astype(o_ref.dtype)


def _gated_conv(u, b_gate, conv_w, seq):
    m, d = u.shape
    ts = min(512, seq)
    tc = min(2048, d)
    rb = ts // SUBLANES
    n_rb = m // SUBLANES
    kern = functools.partial(_conv_kernel, tiles_per_seq=seq // ts)
    return pl.pallas_call(
        kern,
        grid=(m // ts, d // tc),
        in_specs=[pl.BlockSpec((ts, tc), lambda i, j: (i, j)),
                  pl.BlockSpec((SUBLANES, tc), lambda i, j: (jnp.maximum(i * rb - 1, 0), j)),
                  pl.BlockSpec((SUBLANES, tc), lambda i, j: (jnp.minimum((i + 1) * rb, n_rb - 1), j)),
                  pl.BlockSpec((ts, tc), lambda i, j: (i, j)),
                  pl.BlockSpec((3, tc), lambda i, j: (0, j))],
        out_specs=pl.BlockSpec((ts, tc), lambda i, j: (i, j)),
        out_shape=jax.ShapeDtypeStruct((m, d), BF16),
        compiler_params=_params(2),
        name="gated_conv",
    )(u, u, u, b_gate, conv_w)


def _ffn(x, h, gate, wg, wu, wo, seq, side_casts=()):
    m, d = x.shape
    fp = wg.shape[1]
    tm = min(1024, seq)
    tf = min(512, fp)
    tn = min(1024, d)
    tk = _largest_tile(fp, 3072)
    order = sorted(range(len(side_casts)), key=lambda t: -side_casts[t].size)
    steps = {"in": (m // tm) * (fp // tf), "out": (m // tm) * (d // tn) * (fp // tk)}
    host = {t: ("in" if t == order[0] else "out") for t in order}
    jobs = {t: _plain_cast_job(side_casts[t], steps[hst]) for t, hst in host.items()}
    riders = {hst: [t for t in order if host[t] == hst and jobs[t] is not None]
              for hst in ("in", "out")}
    cast = {t: side_casts[t].astype(BF16) for t in order if jobs[t] is None}

    hidden, *riding = _matmul(
        [h], [wg, wu], [], [(jax.ShapeDtypeStruct((m, fp), BF16), _tile_spec(tm, tf))],
        _epi_swiglu, m=m, n=fp, kdim=d, tm=tm, tn=tf, tk=d,
        side_casts=[jobs[t] for t in riders["in"]], name="ffn_in")
    cast.update(zip(riders["in"], riding))
    out, *riding = _matmul(
        [hidden], [wo],
        [(x, _tile_spec(tm, tn)), (gate, _seq_vec_spec(tm, tn, seq))],
        [(jax.ShapeDtypeStruct((m, d), F32), _tile_spec(tm, tn))],
        functools.partial(_epi_residual, 0.5), m=m, n=d, kdim=fp, tm=tm, tn=tn, tk=tk,
        side_casts=[jobs[t] for t in riders["out"]], name="ffn_out")
    cast.update(zip(riders["out"], riding))
    return out, [cast[t] for t in range(len(side_casts))]


def _rope_tables(seq):
    inv_freq = ROPE_THETA ** (-jnp.arange(0, HEAD_DIM, 2, dtype=F32) / HEAD_DIM)
    ang = jnp.arange(seq, dtype=F32)[:, None] * inv_freq[None, :]
    cos = jnp.cos(ang)
    sin = jnp.sin(ang)
    return jnp.concatenate([cos, cos], axis=-1), jnp.concatenate([-sin, sin], axis=-1)


def _mixer(x, h, gate, w, bsz, seq, lam_init, ffn_cast=None):
    m, d = x.shape
    w_in = w["w_in"]
    tm = min(1024, seq)

    tn = min(1024, d)
    tmc = min(512, seq)
    tnc = min(512, d)
    nb = d // tnc
    jobs = [[], [], []]
    if ffn_cast is not None:
        w_fi, w_fo, fl, fj, f, fp = ffn_cast
        qk_steps, conv_steps = (m // tm) * (2 * d // tn), (m // tmc) * nb
        made = [_ffn_in_half_job(w_fi, fl, fj, 0, f, fp, qk_steps),
                _ffn_in_half_job(w_fi, fl, fj, 1, f, fp, qk_steps),
                _ffn_out_job(w_fo, fl, fj, fp, conv_steps)]
        jobs = [[jb] if jb is not None else [] for jb in made]

    cos, sin = _rope_tables(seq)
    rope_spec = pl.BlockSpec((tm, HEAD_DIM), lambda i, j, k: (i % (seq // tm), 0))
    qk = _matmul(
        [h], [w_in], [(cos, rope_spec), (sin, rope_spec)],
        [(jax.ShapeDtypeStruct((m, 2 * d), BF16), _tile_spec(tm, tn))],
        functools.partial(_epi_rope, d // tn, LOG2_E / math.sqrt(HEAD_DIM)),
        m=m, n=2 * d, kdim=d, tm=tm, tn=tn, tk=d, side_casts=jobs[0], name="proj_qk")
    qk, wg = qk if jobs[0] else (qk[0], None)
    v, = _matmul(
        [h], [w_in], [], [(jax.ShapeDtypeStruct((m, d), BF16), _tile_spec(tm, tn))],
        _epi_cast, m=m, n=d, kdim=d, tm=tm, tn=tn, tk=d, b_col_off=(2 * d // tn,), name="proj_v")
    gates = _matmul(
        [h], [w_in], [], [(jax.ShapeDtypeStruct((m, 2 * d), F32), _tile_spec(tm, tn))],
        _epi_sigmoid, m=m, n=2 * d, kdim=d, tm=tm, tn=tn, tk=d, b_col_off=(6 * d // tn,),
        side_casts=jobs[1], name="proj_gates")
    gates, wu = gates if jobs[1] else (gates[0], None)
    b_gate, u, *wo = _matmul(
        [h], [w_in, w_in, w_in], [],
        [(jax.ShapeDtypeStruct((m, d), F32), _tile_spec(tmc, tnc)),
         (jax.ShapeDtypeStruct((m, d), F32), _tile_spec(tmc, tnc))],
        _epi_conv_operands, m=m, n=d, kdim=d, tm=tmc, tn=tnc, tk=d,
        b_col_off=(3 * nb, 4 * nb, 5 * nb), side_casts=jobs[2], name="proj_conv")
    wo = wo[0] if wo else None

    o = _attention(qk, v, w["lambda_qk"], w["g_subln"], bsz=bsz, seq=seq, d=d, lam_init=lam_init)
    z = _gated_conv(u, b_gate, w["conv_w"], seq)

    tmm = min(1024, seq)
    tnm = min(256, d)
    merged, = _matmul(
        [o, z], [w["w_branch_attn"], w["w_branch_conv"]],
        [(gates, _tile_spec(tmm, tnm)), (gates, _tile_spec(tmm, tnm, d // tnm))],
        [(jax.ShapeDtypeStruct((m, d), BF16), _tile_spec(tmm, tnm))],
        _epi_merge, m=m, n=d, kdim=d, tm=tmm, tn=tnm, tk=d, pair_a=(0, 1), name="branch_merge")

    out, = _matmul(
        [merged], [w["w_out"]],
        [(x, _tile_spec(tm, tn)), (gate, _seq_vec_spec(tm, tn, seq))],
        [(jax.ShapeDtypeStruct((m, d), F32), _tile_spec(tm, tn))],
        functools.partial(_epi_residual, 1.0), m=m, n=d, kdim=d, tm=tm, tn=tn, tk=d,
        name="out_proj")
    if ffn_cast is None:
        return out, None
    w_fi, w_fo, fl, fj, f, fp = ffn_cast
    wg = wg if wg is not None else _ffn_in_half(w_fi, fl, fj, 0, f, fp)
    wu = wu if wu is not None else _ffn_in_half(w_fi, fl, fj, 1, f, fp)
    wo = wo if wo is not None else _ffn_out_padded(w_fo, fl, fj, fp)
    return out, (wg, wu, wo)


def _encode(x3, mod, layers, g_final):
    bsz, seq, d = x3.shape
    x = x3.reshape(bsz * seq, d)
    for l, w in enumerate(layers):
        lam_init = 0.8 - 0.6 * math.exp(-0.3 * l)
        md = mod[l].reshape(bsz, N_SUBLAYERS, 3, 1, d)
        shift = [md[:, s, 0] for s in range(N_SUBLAYERS)]
        scale = [md[:, s, 1] for s in range(N_SUBLAYERS)]
        gate = [md[:, s, 2] for s in range(N_SUBLAYERS)]
        h = _norm_mod(x, w["g_norm"][0], scale[0], shift[0], seq)
        pending = [name for name in MIXER_WEIGHTS if w[name].dtype != BF16]
        x, cast = _ffn(x, h, gate[0], *w["ffn"][0], seq, side_casts=[w[name] for name in pending])
        w.update(zip(pending, cast))
        h = _norm_mod(x, w["g_norm"][1], scale[1], shift[1], seq)
        x, ffn2 = _mixer(x, h, gate[1], w, bsz, seq, lam_init,
                         ffn_cast=None if w["ffn"][1] is not None else w["ffn_cast"])
        if ffn2 is not None:
            w["ffn"][1] = ffn2
        h = _norm_mod(x, w["g_norm"][2], scale[2], shift[2], seq)
        x, _ = _ffn(x, h, gate[2], *w["ffn"][1], seq)
    return _final_norm(x, g_final, seq).reshape(bsz, seq, d)


def kernel(x_prompt, x_sample, c_prompt, c_sample, w_mod, b_mod, g_norm, w_ffn_in, w_ffn_out, w_in,
           conv_w, lambda_qk, g_subln, w_branch_attn, w_branch_conv, w_out, g_final):
    depth = w_mod.shape[0]
    ffn_dim = w_ffn_out.shape[2]
    fp = _round_up(ffn_dim, FFN_PAD) if ffn_dim > FFN_PAD else ffn_dim
    bp, bs = c_prompt.shape[0], c_sample.shape[0]
    rows = _round_up(bp + bs, SUBLANES)
    c_all = jnp.pad(jnp.concatenate([c_prompt, c_sample], axis=0), ((0, rows - bp - bs), (0, 0)))

    layers, mod_p, mod_s = [], [], []
    for l in range(depth):
        mod = _modulation(c_all, w_mod[l], b_mod[l])
        mod_p.append(mod[:bp])
        mod_s.append(mod[bp:bp + bs])
        ffn = [(_ffn_in_half(w_ffn_in, l, 0, 0, ffn_dim, fp), _ffn_in_half(w_ffn_in, l, 0, 1, ffn_dim, fp),
                _ffn_out_padded(w_ffn_out, l, 0, fp)), None]
        layers.append(dict(
            ffn=ffn, ffn_cast=(w_ffn_in, w_ffn_out, l, 1, ffn_dim, fp), g_norm=g_norm[l], w_in=w_in[l], conv_w=conv_w[l],
            lambda_qk=lambda_qk[l], g_subln=g_subln[l], w_branch_attn=w_branch_attn[l],
            w_branch_conv=w_branch_conv[l], w_out=w_out[l]))

    y_prompt = _encode(x_prompt, mod_p, layers, g_final)
    y_sample = _encode(x_sample, mod_s, layers, g_final)
    return (y_prompt, y_sample)
```

```python
import functools
import math
from typing import Any, Callable, NamedTuple

import jax
import jax.numpy as jnp
from jax import lax
from jax.experimental import pallas as pl
from jax.experimental.pallas import tpu as pltpu

HEAD_DIM = 128
NORM_EPS = 1e-6
ROPE_THETA = 10000.0
N_SUBLAYERS = 3
LANES = 128
SUBLANES = 8
BF16_SUBLANES = 16
VMEM_LIMIT_BYTES = 60 * 1024 * 1024
FFN_PAD = 1024
LOG2_E = 1.4426950408889634
ATTN_SHORT_SEQ = 4096
ATTN_KEY_CHUNK_SHORT = 1024
ATTN_KEY_CHUNK_LONG = 512
ATTN_KEY_SLABS = 2
V_DOUBLE_BUFFER_MAX_BYTES = 4 * 1024 * 1024
MIXER_WEIGHTS = ("w_in", "w_branch_attn", "w_branch_conv", "w_out")

F32 = jnp.float32
BF16 = jnp.bfloat16


def _params(n_grid_dims, flags=None):
    return pltpu.CompilerParams(
        dimension_semantics=("arbitrary",) * n_grid_dims,
        vmem_limit_bytes=VMEM_LIMIT_BYTES,
        flags=flags,
    )


def _round_up(x, m):
    return (x + m - 1) // m * m


def _largest_tile(total, cap):
    best = LANES
    for t in range(LANES, min(total, cap) + 1, LANES):
        if total % t == 0:
            best = t
    return best


def _cast_pad_cols_kernel(x_ref, o_ref):
    n = x_ref.shape[1]
    o_ref[:, :n] = x_ref[...].astype(o_ref.dtype)
    if o_ref.shape[1] > n:
        o_ref[:, n:] = jnp.zeros((o_ref.shape[0], o_ref.shape[1] - n), o_ref.dtype)


def _cast_pad_rows_kernel(x_ref, o_ref):
    n = x_ref.shape[0]
    o_ref[:n, :] = x_ref[...].astype(o_ref.dtype)
    if o_ref.shape[0] > n:
        o_ref[n:, :] = jnp.zeros((o_ref.shape[0] - n, o_ref.shape[1]), o_ref.dtype)


def _ffn_in_half(w, l, j, half, f, fp):
    d = w.shape[2]
    tr = min(256, d)
    return pl.pallas_call(
        _cast_pad_cols_kernel,
        grid=(d // tr,),
        in_specs=[pl.BlockSpec((None, None, tr, f), lambda i: (l, j, i, half))],
        out_specs=pl.BlockSpec((tr, fp), lambda i: (i, 0)),
        out_shape=jax.ShapeDtypeStruct((d, fp), BF16),
        compiler_params=_params(1),
        name="cast_ffn_in",
    )(w)


def _ffn_out_padded(w, l, j, fp):
    f, d = w.shape[2:]
    tc = min(256, d)
    return pl.pallas_call(
        _cast_pad_rows_kernel,
        grid=(d // tc,),
        in_specs=[pl.BlockSpec((None, None, f, tc), lambda c: (l, j, 0, c))],
        out_specs=pl.BlockSpec((fp, tc), lambda c: (0, c)),
        out_shape=jax.ShapeDtypeStruct((fp, d), BF16),
        compiler_params=_params(1),
        name="cast_ffn_out",
    )(w)


def _mod_kernel(c_ref, w_ref, b_ref, o_ref):
    o_ref[...] = jnp.dot(c_ref[...].astype(BF16), w_ref[...].astype(BF16),
                         preferred_element_type=F32) + b_ref[...]


def _modulation(c_all, w_mod, b_mod):
    rows, d = c_all.shape
    n = w_mod.shape[1]
    tn = min(512, n)
    return pl.pallas_call(
        _mod_kernel,
        grid=(n // tn,),
        in_specs=[pl.BlockSpec((rows, d), lambda j: (0, 0)),
                  pl.BlockSpec((d, tn), lambda j: (0, j)),
                  pl.BlockSpec((1, tn), lambda j: (0, j))],
        out_specs=pl.BlockSpec((rows, tn), lambda j: (0, j)),
        out_shape=jax.ShapeDtypeStruct((rows, n), F32),
        compiler_params=_params(1),
        name="modulation",
    )(c_all, w_mod, b_mod.reshape(1, n))


def _norm_mod_kernel(x_ref, g_ref, scale_ref, shift_ref, o_ref):
    x = x_ref[...]
    ms = jnp.mean(x * x, axis=-1, keepdims=True)
    xn = (x * lax.rsqrt(ms + NORM_EPS)) * g_ref[...]
    o_ref[...] = (xn * (1.0 + scale_ref[...]) + shift_ref[...]).astype(o_ref.dtype)


def _norm_kernel(x_ref, g_ref, o_ref):
    x = x_ref[...]
    ms = jnp.mean(x * x, axis=-1, keepdims=True)
    o_ref[...] = ((x * lax.rsqrt(ms + NORM_EPS)) * g_ref[...]).astype(o_ref.dtype)


def _norm_mod(x, g, scale, shift, seq):
    m, d = x.shape
    tm = min(512, seq)
    vec = pl.BlockSpec((None, 1, d), lambda i: (i * tm // seq, 0, 0))
    return pl.pallas_call(
        _norm_mod_kernel,
        grid=(m // tm,),
        in_specs=[pl.BlockSpec((tm, d), lambda i: (i, 0)),
                  pl.BlockSpec((1, d), lambda i: (0, 0)), vec, vec],
        out_specs=pl.BlockSpec((tm, d), lambda i: (i, 0)),
        out_shape=jax.ShapeDtypeStruct((m, d), BF16),
        compiler_params=_params(1),
        name="norm_mod",
    )(x, g.reshape(1, d), scale, shift)


def _final_norm(x, g, seq):
    m, d = x.shape
    tm = min(512, seq)
    return pl.pallas_call(
        _norm_kernel,
        grid=(m // tm,),
        in_specs=[pl.BlockSpec((tm, d), lambda i: (i, 0)),
                  pl.BlockSpec((1, d), lambda i: (0, 0))],
        out_specs=pl.BlockSpec((tm, d), lambda i: (i, 0)),
        out_shape=jax.ShapeDtypeStruct((m, d), F32),
        compiler_params=_params(1),
        name="final_norm",
    )(x, g.reshape(1, d))


def _mm_body(*refs, n_a, n_b, n_e, side_valid, n_o, nj, nk, pair_a, epilogue):
    n_s = len(side_valid)
    a = refs[:n_a]
    b = refs[n_a:n_a + n_b]
    e = refs[n_a + n_b:n_a + n_b + n_e]
    side_in = refs[n_a + n_b + n_e:n_a + n_b + n_e + n_s]
    n_in = n_a + n_b + n_e + n_s
    o = refs[n_in:n_in + n_o]
    side_out = refs[n_in + n_o:n_in + n_o + n_s]
    acc = refs[n_in + n_o + n_s:]
    ids = (pl.program_id(0), pl.program_id(1))

    step = (ids[0] * nj + ids[1]) * nk + (pl.program_id(2) if nk > 1 else 0)
    for s_in, s_out, n_valid in zip(side_in, side_out, side_valid):
        val = s_in[...].astype(s_out.dtype)
        if n_valid is not None:
            val = jnp.where(step < n_valid, val, jnp.zeros_like(val))
        cols = s_in.shape[-1]
        s_out[:, :cols] = val
        if s_out.shape[-1] > cols:
            s_out[:, cols:] = jnp.zeros((s_out.shape[0], s_out.shape[-1] - cols), s_out.dtype)

    def products():
        return [jnp.dot(a[pair_a[t]][...], b[t][...], preferred_element_type=F32)
                for t in range(n_b)]

    if nk == 1:
        epilogue(ids, products(), e, o)
        return
    k = pl.program_id(2)

    @pl.when(k == 0)
    def _():
        for t, r in enumerate(products()):
            acc[t][...] = r

    @pl.when(jnp.logical_and(k > 0, k < nk - 1))
    def _():
        for t, r in enumerate(products()):
            acc[t][...] += r

    @pl.when(k == nk - 1)
    def _():
        epilogue(ids, [acc[t][...] + r for t, r in enumerate(products())], e, o)


class CastJob(NamedTuple):
    x: Any
    in_block: tuple
    in_index: Callable
    out_shape: tuple
    n_blocks: int
    n_valid: int


def _side_cast_rows(rows, n_steps, also_divides=None):
    for tr in range(BF16_SUBLANES, rows + 1, BF16_SUBLANES):
        if rows % tr == 0 and rows // tr <= n_steps and (also_divides or rows) % tr == 0:
            return tr
    return None


def _plain_cast_job(w, n_steps):
    rows, cols = w.shape
    tr = _side_cast_rows(rows, n_steps)
    if tr is None:
        return None
    return CastJob(w, (tr, cols), lambda b: (b, 0), (rows, cols), rows // tr, rows // tr)


def _ffn_in_half_job(w, l, j, half, f, fp, n_steps):
    d = w.shape[2]
    tr = _side_cast_rows(d, n_steps)
    if tr is None:
        return None
    return CastJob(w, (None, None, tr, f), lambda b: (l, j, b, half), (d, fp), d // tr, d // tr)


def _ffn_out_job(w, l, j, fp, n_steps):
    f, d = w.shape[2:]
    tr = _side_cast_rows(fp, n_steps, also_divides=f)
    if tr is None:
        return None
    n_valid = f // tr
    return CastJob(w, (None, None, tr, d), lambda b: (l, j, jnp.minimum(b, n_valid - 1), 0),
                   (fp, d), fp // tr, n_valid)


def _matmul(a_arrs, b_arrs, extras, outs, epilogue, *, m, n, kdim, tm, tn, tk,
            pair_a=None, b_col_off=None, side_casts=(), name):
    n_b = len(b_arrs)
    pair_a = tuple(pair_a) if pair_a is not None else (0,) * n_b
    b_col_off = tuple(b_col_off) if b_col_off is not None else (0,) * n_b
    assert m % tm == 0 and n % tn == 0 and kdim % tk == 0
    nj, nk = n // tn, kdim // tk
    n_steps = (m // tm) * nj * nk
    in_specs = [pl.BlockSpec((tm, tk), lambda i, j, k: (i, k)) for _ in a_arrs]
    for off in b_col_off:
        in_specs.append(pl.BlockSpec((tk, tn), functools.partial(
            lambda i, j, k, off: (k, j + off), off=off)))
    in_specs += [s for _, s in extras]
    out_specs = [s for _, s in outs]
    out_shape = [sh for sh, _ in outs]
    side_in_specs, side_out_specs = [], []
    for job in side_casts:
        assert job.n_blocks <= n_steps
        block = functools.partial(
            lambda i, j, k, last: jnp.minimum((i * nj + j) * nk + k, last), last=job.n_blocks - 1)
        side_in_specs.append(pl.BlockSpec(
            job.in_block, functools.partial(lambda i, j, k, job, block: job.in_index(block(i, j, k)),
                                            job=job, block=block)))
        side_out_specs.append(pl.BlockSpec(
            (job.in_block[-2], job.out_shape[1]),
            functools.partial(lambda i, j, k, block: (block(i, j, k), 0), block=block)))
        out_shape.append(jax.ShapeDtypeStruct(job.out_shape, BF16))
    side_valid = tuple(None if job.n_valid == job.n_blocks else job.n_valid for job in side_casts)
    scratch = [pltpu.VMEM((tm, tn), F32) for _ in range(n_b)] if nk > 1 else []
    body = functools.partial(_mm_body, n_a=len(a_arrs), n_b=n_b, n_e=len(extras),
                             side_valid=side_valid, n_o=len(outs), nj=nj, nk=nk, pair_a=pair_a,
                             epilogue=epilogue)
    res = pl.pallas_call(
        body,
        grid=(m // tm, nj, nk),
        in_specs=in_specs + side_in_specs,
        out_specs=out_specs + side_out_specs,
        out_shape=out_shape,
        scratch_shapes=scratch,
        compiler_params=_params(3),
        name=name,
    )(*a_arrs, *b_arrs, *[x for x, _ in extras], *[job.x for job in side_casts])
    return res


def _tile_spec(tm, tn, col_off=0):
    return pl.BlockSpec((tm, tn), lambda i, j, k: (i, j + col_off))


def _seq_vec_spec(tm, tn, seq):
    return pl.BlockSpec((None, 1, tn), lambda i, j, k: (i * tm // seq, 0, j))


def _epi_swiglu(ids, r, e, o):
    g, u = r
    o[0][...] = ((g * jax.nn.sigmoid(g)) * u).astype(o[0].dtype)


def _epi_residual(coef, ids, r, e, o):
    x_ref, gate_ref = e
    gate = gate_ref[...] if coef == 1.0 else coef * gate_ref[...]
    o[0][...] = x_ref[...] + gate * r[0]


def _epi_rope(n_q_tiles, q_scale, ids, r, e, o):
    cos = e[0][...]
    sin = e[1][...]
    s = jnp.where(ids[1] < n_q_tiles, q_scale, 1.0).astype(F32)
    acc = r[0]
    for g in range(acc.shape[1] // HEAD_DIM):
        blk = acc[:, g * HEAD_DIM:(g + 1) * HEAD_DIM]
        rot = pltpu.roll(blk, HEAD_DIM // 2, 1)
        o[0][:, g * HEAD_DIM:(g + 1) * HEAD_DIM] = ((blk * cos + rot * sin) * s).astype(o[0].dtype)


def _epi_cast(ids, r, e, o):
    o[0][...] = r[0].astype(o[0].dtype)


def _epi_conv_operands(ids, r, e, o):
    o[0][...] = r[0]
    o[1][...] = r[1] * r[2]


def _epi_sigmoid(ids, r, e, o):
    o[0][...] = jax.nn.sigmoid(r[0])


def _epi_merge(ids, r, e, o):
    o[0][...] = (e[0][...] * r[0] + e[1][...] * r[1]).astype(o[0].dtype)


def _attn_kernel(lq_ref, g_ref, q_ref, k_ref, v_ref, o_ref, vt_ref, acc_ref, sa_ref, sb_ref, *,
                 tk, lam_init):
    n_sub, _, _, tq = acc_ref.shape
    seq = k_ref.shape[0]
    nk = seq // tk

    @pl.when(pl.program_id(2) == 0)
    def _():
        def tr(j, carry):
            start = pl.multiple_of(j * tk, tk)
            vt_ref[j] = v_ref[pl.ds(start, tk), :].astype(F32).T.astype(BF16)
            return carry
        lax.fori_loop(0, nk, tr, 0)

    lq = lq_ref[...]
    lam = (jnp.exp(jnp.sum(lq[0:1] * lq[1:2], axis=-1, keepdims=True))
           - jnp.exp(jnp.sum(lq[2:3] * lq[3:4], axis=-1, keepdims=True)) + lam_init)

    def scores(q, j, s_ref):
        start = pl.multiple_of(j * tk, tk)
        kblk = k_ref[pl.ds(start, tk), :]
        cmax = []
        for c in range(2):
            st = lax.dot_general(kblk[:, c * HEAD_DIM:(c + 1) * HEAD_DIM], q[c],
                                 (((1,), (1,)), ((), ())), preferred_element_type=F32)
            s_ref[c] = st
            cmax.append(jnp.max(st, axis=0, keepdims=True))
        return cmax

    def consume(acc, j, s_ref, cmax, stats):
        vt = vt_ref[j]
        out = []
        for c in range(2):
            m, l = stats[c]
            m_new = jnp.maximum(m, cmax[c])
            alpha = jnp.exp2(m - m_new)
            l_new = alpha * l
            pv = None
            for r in range(ATTN_KEY_SLABS):
                keys = slice(r * tk // ATTN_KEY_SLABS, (r + 1) * tk // ATTN_KEY_SLABS)
                p = jnp.exp2(s_ref[c, keys, :] - m_new)
                l_new = l_new + jnp.sum(p, axis=0, keepdims=True)
                part = jnp.dot(vt[:, keys], p.astype(BF16), preferred_element_type=F32)
                pv = part if pv is None else pv + part
            acc[c] = acc[c] * alpha + pv
            out.append((m_new, l_new))
        return out

    for t in range(n_sub):
        rows = slice(t * tq, (t + 1) * tq)
        q = (q_ref[rows, :HEAD_DIM], q_ref[rows, HEAD_DIM:])
        acc = acc_ref.at[t]
        acc[...] = jnp.zeros(acc.shape, F32)

        def body(jj, carry, q=q, acc=acc):
            stats, cmax_a = carry
            ja = 2 * jj
            cmax_b = scores(q, ja + 1, sb_ref)
            stats = consume(acc, ja, sa_ref, cmax_a, stats)
            cmax_a = scores(q, ja + 2, sa_ref)
            stats = consume(acc, ja + 1, sb_ref, cmax_b, stats)
            return stats, cmax_a

        neg = jnp.full((1, tq), -jnp.inf, F32)
        zero = jnp.zeros((1, tq), F32)
        cmax_a = scores(q, 0, sa_ref)
        stats, cmax_a = lax.fori_loop(0, nk // 2 - 1, body, ([(neg, zero), (neg, zero)], cmax_a),
                                      unroll=4)
        cmax_b = scores(q, nk - 1, sb_ref)
        stats = consume(acc, nk - 2, sa_ref, cmax_a, stats)
        stats = consume(acc, nk - 1, sb_ref, cmax_b, stats)
        l0, l1 = stats[0][1], stats[1][1]

        ot = acc[0] / l0 - lam * (acc[1] / l1)
        ms = jnp.mean(ot * ot, axis=0, keepdims=True)
        o = (ot * lax.rsqrt(ms + NORM_EPS)).T * g_ref[...]
        o_ref[rows, :] = (o * (1.0 - lam_init)).astype(o_ref.dtype)


def _attention(qk, v, lambda_qk, g_subln, *, bsz, seq, d, lam_init):
    heads = d // (2 * HEAD_DIM)
    hw = 2 * HEAD_DIM
    tqs = min(512, seq)
    n_sub = next(n for n in ((4, 2, 1) if seq <= ATTN_SHORT_SEQ else (2, 1)) if seq % (n * tqs) == 0)
    tq = n_sub * tqs
    tk = min(ATTN_KEY_CHUNK_SHORT, seq // 2) if seq <= ATTN_SHORT_SEQ else ATTN_KEY_CHUNK_LONG
    assert seq % (2 * tk) == 0
    nq = seq // tq
    kern = functools.partial(_attn_kernel, tk=tk, lam_init=lam_init)
    v_block_bytes = seq * hw * 2
    v_mode = pl.Buffered(1) if v_block_bytes > V_DOUBLE_BUFFER_MAX_BYTES else None
    return pl.pallas_call(
        kern,
        grid=(bsz, heads, nq),
        in_specs=[pl.BlockSpec((4, HEAD_DIM), lambda b, h, i: (0, 0)),
                  pl.BlockSpec((1, hw), lambda b, h, i: (0, 0)),
                  pl.BlockSpec((tq, hw), lambda b, h, i: (b * nq + i, h)),
                  pl.BlockSpec((seq, hw), lambda b, h, i: (b, heads + h)),
                  pl.BlockSpec((seq, hw), lambda b, h, i: (b, h), pipeline_mode=v_mode)],
        out_specs=pl.BlockSpec((tq, hw), lambda b, h, i: (b * nq + i, h)),
        out_shape=jax.ShapeDtypeStruct((bsz * seq, d), BF16),
        scratch_shapes=[pltpu.VMEM((seq // tk, hw, tk), BF16),
                        pltpu.VMEM((n_sub, 2, hw, tqs), F32),
                        pltpu.VMEM((2, tk, tqs), F32), pltpu.VMEM((2, tk, tqs), F32)],
        compiler_params=_params(3),
        name="diff_attention",
    )(lambda_qk, g_subln.reshape(1, hw), qk, qk, v)


def _conv_kernel(u_ref, prev_ref, next_ref, b_ref, w_ref, o_ref, *, tiles_per_seq):
    i = pl.program_id(0)
    ts = u_ref.shape[0]
    u = u_ref[...]
    pos = i % tiles_per_seq
    prev_row = jnp.where(pos == 0, 0.0, prev_ref[SUBLANES - 1:SUBLANES, :])
    next_row = jnp.where(pos == tiles_per_seq - 1, 0.0, next_ref[0:1, :])
    row = lax.broadcasted_iota(jnp.int32, u.shape, 0)
    u_prev = jnp.where(row == 0, prev_row, pltpu.roll(u, 1, 0))
    u_next = jnp.where(row == ts - 1, next_row, pltpu.roll(u, ts - 1, 0))
    w = w_ref[...]
    y = u_prev * w[0:1] + u * w[1:2] + u_next * w[2:3]
    o_ref[...] = (b_ref[...] * y).astype(o_ref.dtype)


def _gated_conv(u, b_gate, conv_w, seq):
    m, d = u.shape
    ts = min(512, seq)
    tc = min(2048, d)
    rb = ts // SUBLANES
    n_rb = m // SUBLANES
    kern = functools.partial(_conv_kernel, tiles_per_seq=seq // ts)
    return pl.pallas_call(
        kern,
        grid=(m // ts, d // tc),
        in_specs=[pl.BlockSpec((ts, tc), lambda i, j: (i, j)),
                  pl.BlockSpec((SUBLANES, tc), lambda i, j: (jnp.maximum(i * rb - 1, 0), j)),
                  pl.BlockSpec((SUBLANES, tc), lambda i, j: (jnp.minimum((i + 1) * rb, n_rb - 1), j)),
                  pl.BlockSpec((ts, tc), lambda i, j: (i, j)),
                  pl.BlockSpec((3, tc), lambda i, j: (0, j))],
        out_specs=pl.BlockSpec((ts, tc), lambda i, j: (i, j)),
        out_shape=jax.ShapeDtypeStruct((m, d), BF16),
        compiler_params=_params(2),
        name="gated_conv",
    )(u, u, u, b_gate, conv_w)


def _ffn(x, h, gate, wg, wu, wo, seq, side_casts=()):
    m, d = x.shape
    fp = wg.shape[1]
    tm = min(1024, seq)
    tf = min(512, fp)
    tn = min(1024, d)
    tk = _largest_tile(fp, 3072)
    order = sorted(range(len(side_casts)), key=lambda t: -side_casts[t].size)
    steps = {"in": (m // tm) * (fp // tf), "out": (m // tm) * (d // tn) * (fp // tk)}
    host = {t: ("in" if t == order[0] else "out") for t in order}
    jobs = {t: _plain_cast_job(side_casts[t], steps[hst]) for t, hst in host.items()}
    riders = {hst: [t for t in order if host[t] == hst and jobs[t] is not None]
              for hst in ("in", "out")}
    cast = {t: side_casts[t].astype(BF16) for t in order if jobs[t] is None}

    hidden, *riding = _matmul(
        [h], [wg, wu], [], [(jax.ShapeDtypeStruct((m, fp), BF16), _tile_spec(tm, tf))],
        _epi_swiglu, m=m, n=fp, kdim=d, tm=tm, tn=tf, tk=d,
        side_casts=[jobs[t] for t in riders["in"]], name="ffn_in")
    cast.update(zip(riders["in"], riding))
    out, *riding = _matmul(
        [hidden], [wo],
        [(x, _tile_spec(tm, tn)), (gate, _seq_vec_spec(tm, tn, seq))],
        [(jax.ShapeDtypeStruct((m, d), F32), _tile_spec(tm, tn))],
        functools.partial(_epi_residual, 0.5), m=m, n=d, kdim=fp, tm=tm, tn=tn, tk=tk,
        side_casts=[jobs[t] for t in riders["out"]], name="ffn_out")
    cast.update(zip(riders["out"], riding))
    return out, [cast[t] for t in range(len(side_casts))]


def _rope_tables(seq):
    inv_freq = ROPE_THETA ** (-jnp.arange(0, HEAD_DIM, 2, dtype=F32) / HEAD_DIM)
    ang = jnp.arange(seq, dtype=F32)[:, None] * inv_freq[None, :]
    cos = jnp.cos(ang)
    sin = jnp.sin(ang)
    return jnp.concatenate([cos, cos], axis=-1), jnp.concatenate([-sin, sin], axis=-1)


def _mixer(x, h, gate, w, bsz, seq, lam_init, ffn_cast=None):
    m, d = x.shape
    w_in = w["w_in"]
    tm = min(1024, seq)

    tn = min(1024, d)
    tmc = min(512, seq)
    tnc = min(512, d)
    nb = d // tnc
    jobs = [[], [], []]
    if ffn_cast is not None:
        w_fi, w_fo, fl, fj, f, fp = ffn_cast
        qk_steps, conv_steps = (m // tm) * (2 * d // tn), (m // tmc) * nb
        made = [_ffn_in_half_job(w_fi, fl, fj, 0, f, fp, qk_steps),
                _ffn_in_half_job(w_fi, fl, fj, 1, f, fp, qk_steps),
                _ffn_out_job(w_fo, fl, fj, fp, conv_steps)]
        jobs = [[jb] if jb is not None else [] for jb in made]

    cos, sin = _rope_tables(seq)
    rope_spec = pl.BlockSpec((tm, HEAD_DIM), lambda i, j, k: (i % (seq // tm), 0))
    qk = _matmul(
        [h], [w_in], [(cos, rope_spec), (sin, rope_spec)],
        [(jax.ShapeDtypeStruct((m, 2 * d), BF16), _tile_spec(tm, tn))],
        functools.partial(_epi_rope, d // tn, LOG2_E / math.sqrt(HEAD_DIM)),
        m=m, n=2 * d, kdim=d, tm=tm, tn=tn, tk=d, side_casts=jobs[0], name="proj_qk")
    qk, wg = qk if jobs[0] else (qk[0], None)
    v, = _matmul(
        [h], [w_in], [], [(jax.ShapeDtypeStruct((m, d), BF16), _tile_spec(tm, tn))],
        _epi_cast, m=m, n=d, kdim=d, tm=tm, tn=tn, tk=d, b_col_off=(2 * d // tn,), name="proj_v")
    gates = _matmul(
        [h], [w_in], [], [(jax.ShapeDtypeStruct((m, 2 * d), F32), _tile_spec(tm, tn))],
        _epi_sigmoid, m=m, n=2 * d, kdim=d, tm=tm, tn=tn, tk=d, b_col_off=(6 * d // tn,),
        side_casts=jobs[1], name="proj_gates")
    gates, wu = gates if jobs[1] else (gates[0], None)
    b_gate, u, *wo = _matmul(
        [h], [w_in, w_in, w_in], [],
        [(jax.ShapeDtypeStruct((m, d), F32), _tile_spec(tmc, tnc)),
         (jax.ShapeDtypeStruct((m, d), F32), _tile_spec(tmc, tnc))],
        _epi_conv_operands, m=m, n=d, kdim=d, tm=tmc, tn=tnc, tk=d,
        b_col_off=(3 * nb, 4 * nb, 5 * nb), side_casts=jobs[2], name="proj_conv")
    wo = wo[0] if wo else None

    o = _attention(qk, v, w["lambda_qk"], w["g_subln"], bsz=bsz, seq=seq, d=d, lam_init=lam_init)
    z = _gated_conv(u, b_gate, w["conv_w"], seq)

    tmm = min(1024, seq)
    tnm = min(256, d)
    merged, = _matmul(
        [o, z], [w["w_branch_attn"], w["w_branch_conv"]],
        [(gates, _tile_spec(tmm, tnm)), (gates, _tile_spec(tmm, tnm, d // tnm))],
        [(jax.ShapeDtypeStruct((m, d), BF16), _tile_spec(tmm, tnm))],
        _epi_merge, m=m, n=d, kdim=d, tm=tmm, tn=tnm, tk=d, pair_a=(0, 1), name="branch_merge")

    out, = _matmul(
        [merged], [w["w_out"]],
        [(x, _tile_spec(tm, tn)), (gate, _seq_vec_spec(tm, tn, seq))],
        [(jax.ShapeDtypeStruct((m, d), F32), _tile_spec(tm, tn))],
        functools.partial(_epi_residual, 1.0), m=m, n=d, kdim=d, tm=tm, tn=tn, tk=d,
        name="out_proj")
    if ffn_cast is None:
        return out, None
    w_fi, w_fo, fl, fj, f, fp = ffn_cast
    wg = wg if wg is not None else _ffn_in_half(w_fi, fl, fj, 0, f, fp)
    wu = wu if wu is not None else _ffn_in_half(w_fi, fl, fj, 1, f, fp)
    wo = wo if wo is not None else _ffn_out_padded(w_fo, fl, fj, fp)
    return out, (wg, wu, wo)


def _encode(x3, mod, layers, g_final):
    bsz, seq, d = x3.shape
    x = x3.reshape(bsz * seq, d)
    for l, w in enumerate(layers):
        lam_init = 0.8 - 0.6 * math.exp(-0.3 * l)
        md = mod[l].reshape(bsz, N_SUBLAYERS, 3, 1, d)
        shift = [md[:, s, 0] for s in range(N_SUBLAYERS)]
        scale = [md[:, s, 1] for s in range(N_SUBLAYERS)]
        gate = [md[:, s, 2] for s in range(N_SUBLAYERS)]
        h = _norm_mod(x, w["g_norm"][0], scale[0], shift[0], seq)
        pending = [name for name in MIXER_WEIGHTS if w[name].dtype != BF16]
        x, cast = _ffn(x, h, gate[0], *w["ffn"][0], seq, side_casts=[w[name] for name in pending])
        w.update(zip(pending, cast))
        h = _norm_mod(x, w["g_norm"][1], scale[1], shift[1], seq)
        x, ffn2 = _mixer(x, h, gate[1], w, bsz, seq, lam_init,
                         ffn_cast=None if w["ffn"][1] is not None else w["ffn_cast"])
        if ffn2 is not None:
            w["ffn"][1] = ffn2
        h = _norm_mod(x, w["g_norm"][2], scale[2], shift[2], seq)
        x, _ = _ffn(x, h, gate[2], *w["ffn"][1], seq)
    return _final_norm(x, g_final, seq).reshape(bsz, seq, d)


def kernel(x_prompt, x_sample, c_prompt, c_sample, w_mod, b_mod, g_norm, w_ffn_in, w_ffn_out, w_in,
           conv_w, lambda_qk, g_subln, w_branch_attn, w_branch_conv, w_out, g_final):
    depth = w_mod.shape[0]
    ffn_dim = w_ffn_out.shape[2]
    fp = _round_up(ffn_dim, FFN_PAD) if ffn_dim > FFN_PAD else ffn_dim
    bp, bs = c_prompt.shape[0], c_sample.shape[0]
    rows = _round_up(bp + bs, SUBLANES)
    c_all = jnp.pad(jnp.concatenate([c_prompt, c_sample], axis=0), ((0, rows - bp - bs), (0, 0)))

    layers, mod_p, mod_s = [], [], []
    for l in range(depth):
        mod = _modulation(c_all, w_mod[l], b_mod[l])
        mod_p.append(mod[:bp])
        mod_s.append(mod[bp:bp + bs])
        ffn = [(_ffn_in_half(w_ffn_in, l, 0, 0, ffn_dim, fp), _ffn_in_half(w_ffn_in, l, 0, 1, ffn_dim, fp),
                _ffn_out_padded(w_ffn_out, l, 0, fp)), None]
        layers.append(dict(
            ffn=ffn, ffn_cast=(w_ffn_in, w_ffn_out, l, 1, ffn_dim, fp), g_norm=g_norm[l], w_in=w_in[l], conv_w=conv_w[l],
            lambda_qk=lambda_qk[l], g_subln=g_subln[l], w_branch_attn=w_branch_attn[l],
            w_branch_conv=w_branch_conv[l], w_out=w_out[l]))

    y_prompt = _encode(x_prompt, mod_p, layers, g_final)
    y_sample = _encode(x_sample, mod_s, layers, g_final)
    return (y_prompt, y_sample)
```
